```python
import math
import jax
import jax.numpy as jnp
from jax import lax
import numpy as np


D_MODEL = 4096
BATCH = 4
SEQ = 2048
DEPTH = 2
DEC_BATCH = 8
DEC_SEQ = 1
PAST_LEN = 16384
PAGE_SIZE = 128

CONV_W = D_MODEL // 4
CONV_K = 3
H_D = 12
DK = 128
DV = 128
DCONV_K = 4
DELTA_CHUNK = 64
QKV_D = H_D * (2 * DK + DV)
H_F = 12
HD_F = 128
Q_BLOCK = 128
CACHE_FORGET_LOGIT = 8.0
N_EXPERTS = 32
TOP_K = 4
D_FF = 1024
SWIGLU_ALPHA = 1.702
SWIGLU_LIMIT = 7.0
EPS = 1e-6
POOL_NUM = 5
POOL_DEN = 4
IN_SIZES = (CONV_W, CONV_W, CONV_W,
            H_D * DK, H_D * DK, H_D * DV, H_D * DV, H_D, H_D,
            H_F * HD_F, H_F * HD_F, H_F * HD_F, H_F,
            D_MODEL, D_MODEL, D_MODEL)

kernel_name = "hybrid_conv_delta_fox_moe_step"


def rms_norm(x, g):
    xf = x.astype(jnp.float32)
    y = xf * lax.rsqrt(jnp.mean(xf * xf, axis=-1, keepdims=True) + EPS)
    return (y * g.astype(jnp.float32)).astype(x.dtype)


def l2_norm(x):
    xf = x.astype(jnp.float32)
    return xf * lax.rsqrt(jnp.sum(xf * xf, axis=-1, keepdims=True) + EPS)


def causal_dwconv(x, buf, w):
    k = w.shape[0]
    s = x.shape[1]
    xx = jnp.concatenate([buf.astype(x.dtype), x], axis=1)
    y = sum(w[i] * xx[:, i:i + s] for i in range(k))
    return y, xx[:, xx.shape[1] - (k - 1):]


def gated_delta_chunked(q, k, v, beta, g, s0):
    b, s, h, dk = q.shape
    dv = v.shape[-1]
    c = min(DELTA_CHUNK, s)
    n = -(-s // c)
    pad = n * c - s

    def prep(a):
        a = a.astype(jnp.float32)
        if pad:
            a = jnp.pad(a, [(0, 0), (0, pad)] + [(0, 0)] * (a.ndim - 2))
        a = a.reshape((b, n, c) + a.shape[2:])
        return jnp.moveaxis(a, 3, 1)

    q, k, v, beta, g = prep(q), prep(k), prep(v), prep(beta), prep(g)
    G = jnp.cumsum(g, axis=-1)
    tri_strict = jnp.tril(jnp.ones((c, c), bool), -1)
    tri_incl = jnp.tril(jnp.ones((c, c), bool))
    diff = G[..., :, None] - G[..., None, :]
    decay_strict = jnp.exp(jnp.where(tri_strict, diff, -jnp.inf))
    decay_incl = jnp.exp(jnp.where(tri_incl, diff, -jnp.inf))
    k_beta = k * beta[..., None]
    L = jnp.einsum('bhnid,bhnjd->bhnij', k_beta, k) * decay_strict
    eye = jnp.eye(c, dtype=jnp.float32)
    T = lax.linalg.triangular_solve(eye + L, jnp.broadcast_to(eye, L.shape), left_side=True, lower=True)
    u = jnp.einsum('bhnij,bhnje->bhnie', T, v * beta[..., None])
    w = jnp.einsum('bhnij,bhnjd->bhnid', T, k_beta * jnp.exp(G)[..., None])
    attn_intra = jnp.einsum('bhnid,bhnjd->bhnij', q, k) * decay_incl
    q_dec = q * jnp.exp(G)[..., None]
    k_dec = k * jnp.exp(G[..., -1:] - G)[..., None]
    g_last = jnp.exp(G[..., -1])

    def step(S, xs):
        u_c, w_c, a_c, qd_c, kd_c, gl_c = xs
        v_new = u_c - jnp.einsum('bhid,bhde->bhie', w_c, S)
        o = jnp.einsum('bhid,bhde->bhie', qd_c, S) + jnp.einsum('bhij,bhje->bhie', a_c, v_new)
        S = S * gl_c[..., None, None] + jnp.einsum('bhid,bhie->bhde', kd_c, v_new)
        return S, o

    xs = tuple(jnp.moveaxis(a, 2, 0) for a in (u, w, attn_intra, q_dec, k_dec, g_last))
    s_fin, o = lax.scan(step, s0.astype(jnp.float32), xs)
    o = jnp.moveaxis(o, 0, 2).reshape(b, h, n * c, dv)[:, :, :s]
    return jnp.moveaxis(o, 1, 2), s_fin


def fox_attention(q, k, v, fq, fk, q_pos, k_pos):
    b, sq, h, d = q.shape
    scale = d ** -0.5
    fk_t = jnp.moveaxis(fk, 1, 2)

    def block(args):
        qb, fqb, pb = args
        logits = jnp.einsum('bqhd,bkhd->bhqk', qb, k).astype(jnp.float32) * scale
        logits = logits + jnp.moveaxis(fqb, 1, 2)[..., :, None] - fk_t[..., None, :]
        visible = k_pos[None, :] <= pb[:, None]
        logits = jnp.where(visible, logits, -jnp.inf)
        probs = jax.nn.softmax(logits, axis=-1)
        return jnp.einsum('bhqk,bkhd->bqhd', probs.astype(v.dtype), v)

    if sq <= Q_BLOCK:
        return block((q, fq, q_pos))
    pad = (-sq) % Q_BLOCK
    nb = (sq + pad) // Q_BLOCK
    if pad:
        q = jnp.pad(q, ((0, 0), (0, pad), (0, 0), (0, 0)))
        fq = jnp.pad(fq, ((0, 0), (0, pad), (0, 0)))
        q_pos = jnp.pad(q_pos, (0, pad), mode='edge')
    qs = jnp.moveaxis(q.reshape(b, nb, Q_BLOCK, h, d), 1, 0)
    fqs = jnp.moveaxis(fq.reshape(b, nb, Q_BLOCK, h), 1, 0)
    ps = q_pos.reshape(nb, Q_BLOCK)
    out = lax.map(block, (qs, fqs, ps))
    return jnp.moveaxis(out, 0, 1).reshape(b, nb * Q_BLOCK, h, d)[:, :sq]


def moe_ffn(h, lp):
    logits = (jnp.einsum('bsd,de->bse', h, lp['w_router']) + lp['b_router']).astype(jnp.float32)
    top_vals, top_idx = lax.top_k(logits, TOP_K)
    top_w = jax.nn.softmax(top_vals, axis=-1)
    gates = jnp.sum(jax.nn.one_hot(top_idx, N_EXPERTS, dtype=jnp.float32) * top_w[..., None], axis=-2)
    gates = gates.astype(h.dtype)
    x_glu = jnp.einsum('bsd,edf->bsef', h, lp['w_gate']) + lp['b_gate']
    x_lin = jnp.einsum('bsd,edf->bsef', h, lp['w_up']) + lp['b_up']
    x_glu = jnp.minimum(x_glu, SWIGLU_LIMIT)
    x_lin = jnp.clip(x_lin, -SWIGLU_LIMIT, SWIGLU_LIMIT)
    act = x_glu * jax.nn.sigmoid(SWIGLU_ALPHA * x_glu) * (x_lin + 1)
    out = jnp.einsum('bsef,efd->bsd', act * gates[..., None], lp['w_down'])
    return out + jnp.einsum('bse,ed->bsd', gates, lp['b_down'])


def trunk_layer(x, c, lp, sconv_buf, dconv_buf, dstate, past_k, past_v, past_logf):
    f32 = jnp.float32
    b, s, _ = x.shape
    mod = jnp.einsum('bd,de->be', jax.nn.silu(c), lp['w_ada']) + lp['b_ada']
    sh_m, sc_m, gt_m, sh_f, sc_f, gt_f = [m[:, None, :] for m in jnp.split(mod, 6, axis=-1)]
    h = rms_norm(x, lp['g_mix']) * (1 + sc_m) + sh_m
    z = jnp.einsum('bsd,dn->bsn', h, lp['w_in'])
    split_at = np.cumsum(IN_SIZES)[:-1].tolist()
    (a_b, a_c, a_h, d_q, d_k, d_v, d_z, d_beta, d_a,
     f_q, f_k, f_v, f_f, gate_a, gate_b, gate_c) = jnp.split(z, split_at, axis=-1)

    a_conv, new_sconv = causal_dwconv(a_c * a_h, sconv_buf, lp['w_sconv'])
    y_a = a_b * a_conv

    qkv, new_dconv = causal_dwconv(jnp.concatenate([d_q, d_k, d_v], axis=-1), dconv_buf, lp['w_dconv'])
    qkv = jax.nn.silu(qkv)
    q_d, k_d, v_d = jnp.split(qkv, [H_D * DK, 2 * H_D * DK], axis=-1)
    q_d = l2_norm(q_d.reshape(b, s, H_D, DK)) * (DK ** -0.5)
    k_d = l2_norm(k_d.reshape(b, s, H_D, DK))
    v_d = v_d.reshape(b, s, H_D, DV)
    beta = jax.nn.sigmoid(d_beta.astype(f32))
    g = -jnp.exp(lp['a_log'].astype(f32)) * jax.nn.softplus(d_a.astype(f32) + lp['dt_bias'].astype(f32))
    o_d, new_dstate = gated_delta_chunked(q_d, k_d, v_d, beta, g, dstate)
    o_d = rms_norm(o_d, lp['g_dnorm']) * jax.nn.silu(d_z.reshape(b, s, H_D, DV).astype(f32))
    y_b = o_d.reshape(b, s, H_D * DV).astype(x.dtype)

    q_f = rms_norm(f_q.reshape(b, s, H_F, HD_F), lp['g_qnorm'])
    k_f = rms_norm(f_k.reshape(b, s, H_F, HD_F), lp['g_knorm'])
    v_f = f_v.reshape(b, s, H_F, HD_F)
    logf = jax.nn.log_sigmoid(f_f.astype(f32) + lp['b_forget'].astype(f32))
    f_cum = jnp.cumsum(logf, axis=1)
    if past_k is None:
        keys, vals, fk = k_f, v_f, f_cum
        q_pos = jnp.arange(s, dtype=jnp.int32)
        k_pos = q_pos
    else:
        p_len = past_k.shape[1]
        plf = past_logf.astype(f32)
        fk_past = plf - lax.cumsum(plf, axis=1, reverse=True)
        keys = jnp.concatenate([past_k.astype(k_f.dtype), k_f], axis=1)
        vals = jnp.concatenate([past_v.astype(v_f.dtype), v_f], axis=1)
        fk = jnp.concatenate([fk_past, f_cum], axis=1)
        q_pos = p_len + jnp.arange(s, dtype=jnp.int32)
        k_pos = jnp.arange(p_len + s, dtype=jnp.int32)
    o_f = fox_attention(q_f, keys, vals, f_cum, fk, q_pos, k_pos)
    y_c = o_f.reshape(b, s, H_F * HD_F)

    merged = (jax.nn.sigmoid(gate_a) * (y_a @ lp['w_oa'])
              + jax.nn.sigmoid(gate_b) * (y_b @ lp['w_ob'])
              + jax.nn.sigmoid(gate_c) * (y_c @ lp['w_oc']))
    x = x + gt_m * (merged @ lp['w_o'])

    h2 = rms_norm(x, lp['g_ffn']) * (1 + sc_f) + sh_f
    x = x + gt_f * moe_ffn(h2, lp)
    return x, (k_f, v_f, logf, new_dstate, new_dconv, new_sconv)


def setup_inputs(seed: int = 0) -> dict:
    key = jax.random.key(seed)
    keys = iter(jax.random.split(key, 48))
    f32 = jnp.float32
    d = D_MODEL
    n_in = sum(IN_SIZES)
    n_pages = PAST_LEN // PAGE_SIZE
    n_pool = (POOL_NUM * DEC_BATCH * n_pages + POOL_DEN - 1) // POOL_DEN

    def normal(shape, scale=1.0):
        return jax.random.normal(next(keys), shape, f32) * scale

    def gain(shape):
        return 1.0 + 0.05 * normal(shape)

    x_prompt = normal((BATCH, SEQ, d))
    x_sample = normal((DEC_BATCH, DEC_SEQ, d))
    c_prompt = normal((BATCH, d))
    c_sample = normal((DEC_BATCH, d))
    cache_k = normal((DEPTH, n_pool, PAGE_SIZE, H_F, HD_F))
    cache_v = normal((DEPTH, n_pool, PAGE_SIZE, H_F, HD_F))
    cache_logf = jax.nn.log_sigmoid(CACHE_FORGET_LOGIT + 0.5 * normal((DEPTH, n_pool, PAGE_SIZE, H_F)))
    perm = jax.random.permutation(next(keys), n_pool)
    page_table = perm[:DEC_BATCH * n_pages].reshape(DEC_BATCH, n_pages).astype(jnp.int32)
    state_delta = normal((DEPTH, DEC_BATCH, H_D, DK, DV), DK ** -0.5)
    state_delta_conv = normal((DEPTH, DEC_BATCH, DCONV_K - 1, QKV_D))
    state_sconv = normal((DEPTH, DEC_BATCH, CONV_K - 1, CONV_W))

    dt = jnp.exp(jax.random.uniform(next(keys), (DEPTH, H_D), f32, math.log(1e-3), math.log(1e-1)))
    dt_bias = dt + jnp.log(-jnp.expm1(-dt))
    a_log = jnp.log(jax.random.uniform(next(keys), (DEPTH, H_D), f32, 1.0, 16.0))
    b_forget = jax.random.uniform(next(keys), (DEPTH, H_F), f32, 1.0, 4.0)

    return {
        'x_prompt': x_prompt,
        'x_sample': x_sample,
        'c_prompt': c_prompt,
        'c_sample': c_sample,
        'cache_k': cache_k,
        'cache_v': cache_v,
        'cache_logf': cache_logf,
        'page_table': page_table,
        'state_delta': state_delta,
        'state_delta_conv': state_delta_conv,
        'state_sconv': state_sconv,
        'w_ada': normal((DEPTH, d, 6 * d), 0.5 * d ** -0.5),
        'b_ada': normal((DEPTH, 6 * d), 0.01),
        'g_mix': gain((DEPTH, d)),
        'w_in': normal((DEPTH, d, n_in), d ** -0.5),
        'w_sconv': normal((DEPTH, CONV_K, CONV_W), CONV_K ** -0.5),
        'w_dconv': normal((DEPTH, DCONV_K, QKV_D), DCONV_K ** -0.5),
        'a_log': a_log,
        'dt_bias': dt_bias,
        'g_dnorm': gain((DEPTH, DV)),
        'g_qnorm': gain((DEPTH, HD_F)),
        'g_knorm': gain((DEPTH, HD_F)),
        'b_forget': b_forget,
        'w_oa': normal((DEPTH, CONV_W, d), CONV_W ** -0.5),
        'w_ob': normal((DEPTH, H_D * DV, d), (H_D * DV) ** -0.5),
        'w_oc': normal((DEPTH, H_F * HD_F, d), (H_F * HD_F) ** -0.5),
        'w_o': normal((DEPTH, d, d), d ** -0.5),
        'g_ffn': gain((DEPTH, d)),
        'w_router': normal((DEPTH, d, N_EXPERTS), d ** -0.5),
        'b_router': normal((DEPTH, N_EXPERTS), 0.01),
        'w_gate': normal((DEPTH, N_EXPERTS, d, D_FF), d ** -0.5),
        'b_gate': normal((DEPTH, N_EXPERTS, D_FF), 0.01),
        'w_up': normal((DEPTH, N_EXPERTS, d, D_FF), d ** -0.5),
        'b_up': normal((DEPTH, N_EXPERTS, D_FF), 0.01),
        'w_down': normal((DEPTH, N_EXPERTS, D_FF, d), D_FF ** -0.5),
        'b_down': normal((DEPTH, N_EXPERTS, d), 0.01),
    }


def reference(x_prompt, x_sample, c_prompt, c_sample, cache_k, cache_v, cache_logf, page_table,
              state_delta, state_delta_conv, state_sconv,
              w_ada, b_ada, g_mix, w_in, w_sconv, w_dconv, a_log, dt_bias, g_dnorm, g_qnorm, g_knorm,
              b_forget, w_oa, w_ob, w_oc, w_o, g_ffn, w_router, b_router, w_gate, b_gate, w_up, b_up,
              w_down, b_down):
    bp = x_prompt.shape[0]
    db = x_sample.shape[0]
    n_pages = page_table.shape[1]
    past_len = n_pages * PAGE_SIZE
    xp, xs = x_prompt, x_sample
    zero_sconv = jnp.zeros((bp, CONV_K - 1, CONV_W), x_prompt.dtype)
    zero_dconv = jnp.zeros((bp, DCONV_K - 1, QKV_D), x_prompt.dtype)
    zero_delta = jnp.zeros((bp, H_D, DK, DV), jnp.float32)
    st_p = []
    st_s = []
    for l in range(DEPTH):
        lp = dict(w_ada=w_ada[l], b_ada=b_ada[l], g_mix=g_mix[l], w_in=w_in[l], w_sconv=w_sconv[l],
                  w_dconv=w_dconv[l], a_log=a_log[l], dt_bias=dt_bias[l], g_dnorm=g_dnorm[l],
                  g_qnorm=g_qnorm[l], g_knorm=g_knorm[l], b_forget=b_forget[l], w_oa=w_oa[l],
                  w_ob=w_ob[l], w_oc=w_oc[l], w_o=w_o[l], g_ffn=g_ffn[l], w_router=w_router[l],
                  b_router=b_router[l], w_gate=w_gate[l], b_gate=b_gate[l], w_up=w_up[l],
                  b_up=b_up[l], w_down=w_down[l], b_down=b_down[l])
        xp, sp = trunk_layer(xp, c_prompt, lp, zero_sconv, zero_dconv, zero_delta, None, None, None)
        past_k = cache_k[l][page_table].reshape(db, past_len, H_F, HD_F)
        past_v = cache_v[l][page_table].reshape(db, past_len, H_F, HD_F)
        past_logf = cache_logf[l][page_table].reshape(db, past_len, H_F)
        xs, ss = trunk_layer(xs, c_sample, lp, state_sconv[l], state_delta_conv[l], state_delta[l],
                             past_k, past_v, past_logf)
        st_p.append(sp)
        st_s.append(ss)

    def stack(sts, i):
        return jnp.stack([st[i] for st in sts], axis=0)

    return (xp, xs,
            stack(st_p, 0), stack(st_p, 1), stack(st_p, 2), stack(st_p, 3), stack(st_p, 4), stack(st_p, 5),
            stack(st_s, 0), stack(st_s, 1), stack(st_s, 2), stack(st_s, 3), stack(st_s, 4), stack(st_s, 5))
```

```python
import functools

import jax
import jax.numpy as jnp
import numpy as np
from jax import lax
from jax.experimental import pallas as pl
from jax.experimental.pallas import tpu as pltpu

F32 = jnp.float32
BF16 = jnp.bfloat16
I32 = jnp.int32
U32 = jnp.uint32
HI = lax.Precision.HIGHEST

EPS = 1e-6
TOP_K = 4
DELTA_CHUNK = 64
SWIGLU_ALPHA = 1.702
SWIGLU_LIMIT = 7.0
LANES = 128
SUBLANES = 8
TM = 256
NEG = -1e30
VMEM_LIMIT = 56 * 1024 * 1024


def _cparams(n_axes, vmem=None):
    return pltpu.CompilerParams(dimension_semantics=("arbitrary",) * n_axes,
                                vmem_limit_bytes=vmem or VMEM_LIMIT)


def _dot(a, b, precision=None):
    return jnp.dot(a, b, preferred_element_type=F32, precision=precision)


def _dot_nt(a, b, precision=None):
    return lax.dot_general(a, b, (((1,), (1,)), ((), ())), preferred_element_type=F32, precision=precision)


def _dot_tn(a, b, precision=None):
    return lax.dot_general(a, b, (((0,), (0,)), ((), ())), preferred_element_type=F32, precision=precision)


def _tile(n, preferred):
    t = preferred
    while n % t:
        t //= 2
    assert t >= LANES, (n, preferred)
    return t


def _iota(shape, dim):
    return lax.broadcasted_iota(I32, shape, dim)


def _sigmoid(x):
    return 1.0 / (1.0 + jnp.exp(-x))


def _ada_kernel(c_ref, w_ref, b_ref, o_ref):
    c = c_ref[...]
    a = (c * _sigmoid(c)).astype(BF16)
    o_ref[...] = _dot(a, w_ref[...].astype(BF16)) + b_ref[...]


def _ada(c_rows, w_ada, b_ada):
    n_layers, d, n = w_ada.shape
    tn = _tile(n, 512)
    return pl.pallas_call(
        _ada_kernel,
        grid=(n_layers, n // tn),
        in_specs=[pl.BlockSpec((16, d), lambda l, j: (0, 0)),
                  pl.BlockSpec((None, d, tn), lambda l, j: (l, 0, j)),
                  pl.BlockSpec((None, 1, tn), lambda l, j: (l, 0, j))],
        out_specs=pl.BlockSpec((None, 16, tn), lambda l, j: (l, 0, j)),
        out_shape=jax.ShapeDtypeStruct((n_layers, 16, n), F32),
        compiler_params=_cparams(2),
    )(c_rows, w_ada, b_ada.reshape(n_layers, 1, n))


def _modulated_norm(x, g, sc, sh):
    ms = jnp.mean(x * x, axis=-1, keepdims=True)
    y = x * lax.rsqrt(ms + EPS) * g
    tm, d = y.shape
    y3 = y.reshape(tm // SUBLANES, SUBLANES, d)
    return (y3 * (1.0 + sc[None]) + sh[None]).reshape(tm, d)


def _norm_kernel(x_ref, g_ref, sc_ref, sh_ref, o_ref):
    o_ref[...] = _modulated_norm(x_ref[...], g_ref[...], sc_ref[...], sh_ref[...]).astype(o_ref.dtype)


def _mod_index(tiles_per_seq, n_prompt):
    return lambda i: (jnp.minimum(i // tiles_per_seq, n_prompt), 0, 0)


def _norm(x, g, sc, sh, tiles_per_seq):
    r, d = x.shape
    n_prompt = sc.shape[0] - 1
    mod_spec = pl.BlockSpec((None, SUBLANES, d), _mod_index(tiles_per_seq, n_prompt))
    return pl.pallas_call(
        _norm_kernel,
        grid=(r // TM,),
        in_specs=[pl.BlockSpec((TM, d), lambda i: (i, 0)),
                  pl.BlockSpec((1, d), lambda i: (0, 0)),
                  mod_spec, mod_spec],
        out_specs=pl.BlockSpec((TM, d), lambda i: (i, 0)),
        out_shape=jax.ShapeDtypeStruct((r, d), BF16),
        compiler_params=_cparams(1),
    )(x, g.reshape(1, d), sc, sh)


def _norm_router_kernel(n_experts, x_ref, g_ref, sc_ref, sh_ref, wr_ref, br_ref, hp_ref, idx_ref, wt_ref):
    h = _modulated_norm(x_ref[...], g_ref[...], sc_ref[...], sh_ref[...])
    tm, d = h.shape
    lo = pltpu.bitcast(h[:, :d // 2].astype(BF16).astype(F32), U32)
    hi = pltpu.bitcast(h[:, d // 2:].astype(BF16).astype(F32), U32)
    hp_ref[...] = (hi & jnp.uint32(0xFFFF0000)) | (lo >> 16)
    logits = _dot(h, wr_ref[...], HI) + br_ref[...]
    lane = _iota(logits.shape, 1)
    work = jnp.where(lane < n_experts, logits, NEG)
    vals, idxs = [], []
    for _ in range(TOP_K):
        m = jnp.max(work, axis=-1, keepdims=True)
        ix = jnp.min(jnp.where(work == m, lane, LANES), axis=-1, keepdims=True)
        vals.append(m)
        idxs.append(ix)
        work = jnp.where(lane == ix, NEG, work)
    es = [jnp.exp(v - vals[0]) for v in vals]
    tot = es[0] + es[1] + es[2] + es[3]
    idx_out = jnp.zeros(logits.shape, I32)
    wt_out = jnp.zeros(logits.shape, F32)
    for k in range(TOP_K):
        idx_out = jnp.where(lane == k, idxs[k], idx_out)
        wt_out = jnp.where(lane == k, es[k] / tot, wt_out)
    idx_ref[...] = idx_out
    wt_ref[...] = wt_out


def _norm_router(x, g, sc, sh, w_router, b_router, tiles_per_seq):
    r, d = x.shape
    n_experts = w_router.shape[1]
    n_prompt = sc.shape[0] - 1
    wr = jnp.pad(w_router, ((0, 0), (0, LANES - n_experts)))
    br = jnp.pad(b_router, (0, LANES - n_experts)).reshape(1, LANES)
    mod_spec = pl.BlockSpec((None, SUBLANES, d), _mod_index(tiles_per_seq, n_prompt))
    return pl.pallas_call(
        functools.partial(_norm_router_kernel, n_experts),
        grid=(r // TM,),
        in_specs=[pl.BlockSpec((TM, d), lambda i: (i, 0)),
                  pl.BlockSpec((1, d), lambda i: (0, 0)),
                  mod_spec, mod_spec,
                  pl.BlockSpec((d, LANES), lambda i: (0, 0)),
                  pl.BlockSpec((1, LANES), lambda i: (0, 0))],
        out_specs=[pl.BlockSpec((TM, d // 2), lambda i: (i, 0)),
                   pl.BlockSpec((TM, LANES), lambda i: (i, 0)),
                   pl.BlockSpec((TM, LANES), lambda i: (i, 0))],
        out_shape=[jax.ShapeDtypeStruct((r, d // 2), U32),
                   jax.ShapeDtypeStruct((r, LANES), I32),
                   jax.ShapeDtypeStruct((r, LANES), F32)],
        compiler_params=_cparams(1),
    )(x, g.reshape(1, d), sc, sh, wr, br)


def _mm_kernel(x_ref, w_ref, o_ref):
    o_ref[...] = _dot(x_ref[...], w_ref[...]).astype(o_ref.dtype)


def _mm(x, w, out_dtype=F32):
    r, k = x.shape
    n = w.shape[1]
    tn = _tile(n, 512)
    return pl.pallas_call(
        _mm_kernel,
        grid=(n // tn, r // TM),
        in_specs=[pl.BlockSpec((TM, k), lambda j, i: (i, 0)),
                  pl.BlockSpec((k, tn), lambda j, i: (0, j))],
        out_specs=pl.BlockSpec((TM, tn), lambda j, i: (i, j)),
        out_shape=jax.ShapeDtypeStruct((r, n), out_dtype),
        compiler_params=_cparams(2),
    )(x, w)


def _act_kernel(nh_d, nh_f, z_ref, p_ref, o_ref):
    z = z_ref[...]
    lane = _iota(z.shape, 1)
    beta = _sigmoid(z)
    za = z + p_ref[1:2, :]
    softplus = jnp.maximum(za, 0.0) + jnp.log1p(jnp.exp(-jnp.abs(za)))
    g = -jnp.exp(p_ref[0:1, :]) * softplus
    zf = z + p_ref[2:3, :]
    logf = jnp.minimum(zf, 0.0) - jnp.log1p(jnp.exp(-jnp.abs(zf)))
    o_ref[...] = jnp.where(lane < nh_d, beta, jnp.where(lane < 2 * nh_d, g, logf))


def _gate_acts(z_small, a_log, dt_bias, b_forget):
    r = z_small.shape[0]
    nh_d, nh_f = a_log.shape[0], b_forget.shape[0]
    p = jnp.zeros((SUBLANES, LANES), F32)
    p = p.at[0, nh_d:2 * nh_d].set(a_log).at[1, nh_d:2 * nh_d].set(dt_bias)
    p = p.at[2, 2 * nh_d:2 * nh_d + nh_f].set(b_forget)
    return pl.pallas_call(
        functools.partial(_act_kernel, nh_d, nh_f),
        grid=(r // TM,),
        in_specs=[pl.BlockSpec((TM, LANES), lambda i: (i, 0)),
                  pl.BlockSpec((SUBLANES, LANES), lambda i: (0, 0))],
        out_specs=pl.BlockSpec((TM, LANES), lambda i: (i, 0)),
        out_shape=jax.ShapeDtypeStruct((r, LANES), F32),
        compiler_params=_cparams(1),
    )(z_small, p)


def _conv_prompt_kernel(taps, gated, *refs):
    if gated:
        ab_ref, ac_ref, ah_ref, w_ref, y_ref, st_ref, ubuf = refs
    else:
        x_ref, w_ref, y_ref, st_ref, ubuf = refs
    s = pl.program_id(2)
    ts = y_ref.shape[0]

    @pl.when(s == 0)
    def _():
        ubuf[0:SUBLANES, :] = jnp.zeros((SUBLANES, ubuf.shape[1]), F32)

    u = ac_ref[...] * ah_ref[...] if gated else x_ref[...]
    ubuf[SUBLANES:SUBLANES + ts, :] = u
    base = SUBLANES - (taps - 1)
    acc = w_ref[0:1, :] * ubuf[base:base + ts, :]
    for i in range(1, taps):
        acc = acc + w_ref[i:i + 1, :] * ubuf[base + i:base + i + ts, :]
    y = ab_ref[...] * acc if gated else acc * _sigmoid(acc)
    y_ref[...] = y.astype(y_ref.dtype)
    tail = ubuf[ts:ts + SUBLANES, :]
    ubuf[0:SUBLANES, :] = tail

    @pl.when(s == pl.num_programs(2) - 1)
    def _():
        st_ref[...] = tail


def _conv_prompt(z, w, col0, width, n_batch, seq, gated, out_dtype):
    taps = w.shape[0]
    tc = _tile(np.gcd(width, col0) if col0 else width, 512)
    ts = TM
    nt = seq // ts
    grid = (width // tc, n_batch, nt)
    row = lambda j, b, s: b * nt + s
    if gated:
        ins = [pl.BlockSpec((ts, tc), lambda j, b, s, o=o: (row(j, b, s), (col0 + o * width) // tc + j))
               for o in range(3)]
        args = (z, z, z, w)
    else:
        ins = [pl.BlockSpec((ts, tc), lambda j, b, s: (row(j, b, s), col0 // tc + j))]
        args = (z, w)
    ins.append(pl.BlockSpec((taps, tc), lambda j, b, s: (0, j)))
    return pl.pallas_call(
        functools.partial(_conv_prompt_kernel, taps, gated),
        grid=grid,
        in_specs=ins,
        out_specs=[pl.BlockSpec((ts, tc), lambda j, b, s: (row(j, b, s), j)),
                   pl.BlockSpec((None, SUBLANES, tc), lambda j, b, s: (b, 0, j))],
        out_shape=[jax.ShapeDtypeStruct((n_batch * seq, width), out_dtype),
                   jax.ShapeDtypeStruct((n_batch, SUBLANES, width), F32)],
        scratch_shapes=[pltpu.VMEM((ts + SUBLANES, tc), F32)],
        compiler_params=_cparams(3),
    )(*args)


def _conv_step_kernel(taps, gated, *refs):
    if gated:
        ab_ref, ac_ref, ah_ref, w_ref, buf_ref, y_ref, nb_ref = refs
        u = ac_ref[...] * ah_ref[...]
    else:
        x_ref, w_ref, buf_ref, y_ref, nb_ref = refs
        u = x_ref[...]
    acc = w_ref[taps - 1:taps, :] * u
    for i in range(taps - 1):
        acc = acc + w_ref[i:i + 1, :] * buf_ref[i]
    y = ab_ref[...] * acc if gated else acc * _sigmoid(acc)
    y_ref[...] = y.astype(y_ref.dtype)
    for i in range(taps - 2):
        nb_ref[i] = buf_ref[i + 1]
    nb_ref[taps - 2] = u


def _conv_step(z_rows, w, buf, col0, width, gated, out_dtype):
    taps = w.shape[0]
    tc = _tile(np.gcd(width, col0) if col0 else width, 512)
    nb = z_rows.shape[0]
    buf_t = jnp.transpose(buf, (1, 0, 2))
    if gated:
        ins = [pl.BlockSpec((nb, tc), lambda j, o=o: (0, (col0 + o * width) // tc + j)) for o in range(3)]
        args = (z_rows, z_rows, z_rows, w, buf_t)
    else:
        ins = [pl.BlockSpec((nb, tc), lambda j: (0, col0 // tc + j))]
        args = (z_rows, w, buf_t)
    ins += [pl.BlockSpec((taps, tc), lambda j: (0, j)),
            pl.BlockSpec((taps - 1, nb, tc), lambda j: (0, 0, j))]
    y, nbuf = pl.pallas_call(
        functools.partial(_conv_step_kernel, taps, gated),
        grid=(width // tc,),
        in_specs=ins,
        out_specs=[pl.BlockSpec((nb, tc), lambda j: (0, j)),
                   pl.BlockSpec((taps - 1, nb, tc), lambda j: (0, 0, j))],
        out_shape=[jax.ShapeDtypeStruct((nb, width), out_dtype),
                   jax.ShapeDtypeStruct((taps - 1, nb, width), F32)],
        compiler_params=_cparams(1),
    )(*args)
    return y, jnp.transpose(nbuf, (1, 0, 2))


def _delta_kernel(nh, q_ref, k_ref, v_ref, act_ref, dz_ref, s0_ref, gn_ref, o_ref, sf_ref, state):
    h = pl.program_id(1)
    c = pl.program_id(2)
    ch, dk = q_ref.shape

    @pl.when(c == 0)
    def _():
        state[...] = s0_ref[...]

    q = q_ref[...]
    k = k_ref[...]
    q = q * lax.rsqrt(jnp.sum(q * q, axis=-1, keepdims=True) + EPS) * (dk ** -0.5)
    k = k * lax.rsqrt(jnp.sum(k * k, axis=-1, keepdims=True) + EPS)
    act = act_ref[...]
    sel_row = _iota((LANES, LANES), 0)
    beta_b = _dot(act, (sel_row == h).astype(F32), HI)
    g_b = _dot(act, (sel_row == nh + h).astype(F32), HI)
    ii = _iota((ch, ch), 0)
    jj = _iota((ch, ch), 1)
    cum_b = _dot((ii >= jj).astype(F32), g_b, HI)
    pick0 = (_iota((ch, LANES), 1) == 0).astype(F32)
    cum_r = _dot_nt(pick0, cum_b, HI)
    diff = cum_b[:, :ch] - cum_r
    decay_strict = jnp.exp(jnp.where(ii > jj, diff, NEG))
    decay_incl = jnp.exp(jnp.where(ii >= jj, diff, NEG))
    kb = k * beta_b
    kbf = k.astype(BF16)
    lmat = _dot_nt(kb.astype(BF16), kbf) * decay_strict
    eye = (ii == jj).astype(F32)
    tmat = eye - lmat
    pw = lmat
    n_sq = int(np.log2(ch)) - 1
    for _ in range(n_sq):
        pw = _dot(pw, pw, HI)
        tmat = tmat + _dot(tmat, pw, HI)
    tb = tmat.astype(BF16)
    e_cum = jnp.exp(cum_b)
    u = _dot(tb, (v_ref[...] * beta_b).astype(BF16))
    w = _dot(tb, (kb * e_cum).astype(BF16))
    attn = _dot_nt(q.astype(BF16), kbf) * decay_incl
    cum_last = cum_b[ch - 1:ch, :]
    s_prev = state[...]
    sb = s_prev.astype(BF16)
    v_new = u - _dot(w.astype(BF16), sb)
    o = _dot((q * e_cum).astype(BF16), sb) + _dot(attn.astype(BF16), v_new.astype(BF16))
    k_dec = k * jnp.exp(cum_last - cum_b)
    state[...] = s_prev * jnp.exp(cum_last) + _dot_tn(k_dec.astype(BF16), v_new.astype(BF16))
    dz = dz_ref[...]
    on = o * lax.rsqrt(jnp.mean(o * o, axis=-1, keepdims=True) + EPS) * gn_ref[...]
    o_ref[...] = (on * (dz * _sigmoid(dz))).astype(o_ref.dtype)

    @pl.when(c == pl.num_programs(2) - 1)
    def _():
        sf_ref[...] = state[...]


def _delta(qkv, act, dz_arr, dz_col0, s0, g_dnorm, n_batch, seq):
    nh, dk, dv = s0.shape[1], s0.shape[2], s0.shape[3]
    ch = DELTA_CHUNK
    nc = seq // ch
    row = lambda b, h, c: b * nc + c
    return pl.pallas_call(
        functools.partial(_delta_kernel, nh),
        grid=(n_batch, nh, nc),
        in_specs=[pl.BlockSpec((ch, dk), lambda b, h, c: (row(b, h, c), h)),
                  pl.BlockSpec((ch, dk), lambda b, h, c: (row(b, h, c), nh + h)),
                  pl.BlockSpec((ch, dv), lambda b, h, c: (row(b, h, c), 2 * nh + h)),
                  pl.BlockSpec((ch, LANES), lambda b, h, c: (row(b, h, c), 0)),
                  pl.BlockSpec((ch, dv), lambda b, h, c: (row(b, h, c), dz_col0 // dv + h)),
                  pl.BlockSpec((None, None, dk, dv), lambda b, h, c: (b, h, 0, 0)),
                  pl.BlockSpec((1, dv), lambda b, h, c: (0, 0))],
        out_specs=[pl.BlockSpec((ch, dv), lambda b, h, c: (row(b, h, c), h)),
                   pl.BlockSpec((None, None, dk, dv), lambda b, h, c: (b, h, 0, 0))],
        out_shape=[jax.ShapeDtypeStruct((n_batch * seq, nh * dv), BF16),
                   jax.ShapeDtypeStruct(s0.shape, F32)],
        scratch_shapes=[pltpu.VMEM((dk, dv), F32)],
        compiler_params=_cparams(3),
    )(qkv, qkv, qkv, act, dz_arr, s0, g_dnorm.reshape(1, dv))


def _qkprep_kernel(nh, hd, lf0, tiles_per_seq, q_ref, k_ref, act_ref, gq_ref, gk_ref,
                   qn_ref, kn_ref, knb_ref, fc_ref, carry):
    i = pl.program_id(0)
    scale = hd ** -0.5
    for h in range(nh):
        sl = slice(h * hd, (h + 1) * hd)
        q = q_ref[:, sl]
        qn = q * lax.rsqrt(jnp.mean(q * q, axis=-1, keepdims=True) + EPS) * gq_ref[...]
        qn_ref[:, sl] = (qn * scale).astype(BF16)
        k = k_ref[:, sl]
        kn = k * lax.rsqrt(jnp.mean(k * k, axis=-1, keepdims=True) + EPS) * gk_ref[...]
        kn_ref[:, sl] = kn
        knb_ref[:, sl] = kn.astype(BF16)

    @pl.when(i % tiles_per_seq == 0)
    def _():
        carry[...] = jnp.zeros(carry.shape, F32)

    tm = act_ref.shape[0]
    tri = (_iota((tm, tm), 0) >= _iota((tm, tm), 1)).astype(F32)
    fc = _dot(tri, act_ref[...], HI) + carry[0:1, :]
    fc_ref[...] = fc
    carry[...] = jnp.broadcast_to(fc[tm - 1:tm, :], carry.shape)


def _qkprep(z_f, act, g_q, g_k, nh, tiles_per_seq):
    r = z_f.shape[0]
    hd = g_q.shape[0]
    w = nh * hd
    return pl.pallas_call(
        functools.partial(_qkprep_kernel, nh, hd, 0, tiles_per_seq),
        grid=(r // TM,),
        in_specs=[pl.BlockSpec((TM, w), lambda i: (i, 0)),
                  pl.BlockSpec((TM, w), lambda i: (i, 1)),
                  pl.BlockSpec((TM, LANES), lambda i: (i, 0)),
                  pl.BlockSpec((1, hd), lambda i: (0, 0)),
                  pl.BlockSpec((1, hd), lambda i: (0, 0))],
        out_specs=[pl.BlockSpec((TM, w), lambda i: (i, 0)),
                   pl.BlockSpec((TM, w), lambda i: (i, 0)),
                   pl.BlockSpec((TM, w), lambda i: (i, 0)),
                   pl.BlockSpec((TM, LANES), lambda i: (i, 0))],
        out_shape=[jax.ShapeDtypeStruct((r, w), BF16),
                   jax.ShapeDtypeStruct((r, w), F32),
                   jax.ShapeDtypeStruct((r, w), BF16),
                   jax.ShapeDtypeStruct((r, LANES), F32)],
        scratch_shapes=[pltpu.VMEM((SUBLANES, LANES), F32)],
        compiler_params=_cparams(1),
    )(z_f, z_f, act, g_q.reshape(1, hd), g_k.reshape(1, hd))


def _flash_kernel(lane0, q_ref, k_ref, v_ref, fq_ref, fk_ref, o_ref, m_ref, l_ref, acc_ref, fqb_ref):
    h = pl.program_id(1)
    qi = pl.program_id(2)
    ki = pl.program_id(3)
    tq = q_ref.shape[0]
    tk = k_ref.shape[0]

    @pl.when(ki == 0)
    def _():
        m_ref[...] = jnp.full(m_ref.shape, NEG, F32)
        l_ref[...] = jnp.zeros(l_ref.shape, F32)
        acc_ref[...] = jnp.zeros(acc_ref.shape, F32)
        sel = (_iota((LANES, LANES), 0) == lane0 + h).astype(F32)
        fqb_ref[...] = _dot(fq_ref[...], sel, HI)

    @pl.when(ki <= qi)
    def _():
        s = _dot_nt(q_ref[...], k_ref[...])
        pick = (_iota((SUBLANES, LANES), 1) == lane0 + h).astype(F32)
        fk_row = _dot_nt(pick, fk_ref[...], HI)[0:1, :]
        fq = fqb_ref[...]
        s = s + jnp.concatenate([fq] * (tk // LANES), axis=1) - fk_row
        rows = qi * tq + _iota((tq, tk), 0)
        cols = ki * tk + _iota((tq, tk), 1)
        s = jnp.where(cols <= rows, s, NEG)
        m_prev = m_ref[...]
        m_new = jnp.maximum(m_prev, jnp.max(s, axis=-1, keepdims=True))
        alpha = jnp.exp(m_prev - m_new)
        p = jnp.exp(s - m_new)
        l_ref[...] = alpha * l_ref[...] + jnp.sum(p, axis=-1, keepdims=True)
        acc_ref[...] = alpha * acc_ref[...] + _dot(p.astype(BF16), v_ref[...].astype(BF16))
        m_ref[...] = m_new

    @pl.when(ki == pl.num_programs(3) - 1)
    def _():
        o_ref[...] = (acc_ref[...] / l_ref[...]).astype(o_ref.dtype)


def _flash(qn, knb, z_f, fcum, nh, lane0, n_batch, seq, tq):
    hd = qn.shape[1] // nh
    nq = seq // tq
    qrow = lambda b, h, qi, ki: b * nq + qi
    krow = lambda b, h, qi, ki: b * nq + jnp.minimum(ki, qi)
    return pl.pallas_call(
        functools.partial(_flash_kernel, lane0),
        grid=(n_batch, nh, nq, nq),
        in_specs=[pl.BlockSpec((tq, hd), lambda b, h, qi, ki: (qrow(b, h, qi, ki), h)),
                  pl.BlockSpec((tq, hd), lambda b, h, qi, ki: (krow(b, h, qi, ki), h)),
                  pl.BlockSpec((tq, hd), lambda b, h, qi, ki: (krow(b, h, qi, ki), 2 * nh + h)),
                  pl.BlockSpec((tq, LANES), lambda b, h, qi, ki: (qrow(b, h, qi, ki), 0)),
                  pl.BlockSpec((tq, LANES), lambda b, h, qi, ki: (krow(b, h, qi, ki), 0))],
        out_specs=pl.BlockSpec((tq, hd), lambda b, h, qi, ki: (qrow(b, h, qi, ki), h)),
        out_shape=jax.ShapeDtypeStruct((n_batch * seq, nh * hd), BF16),
        scratch_shapes=[pltpu.VMEM((tq, 1), F32), pltpu.VMEM((tq, 1), F32),
                        pltpu.VMEM((tq, hd), F32), pltpu.VMEM((tq, LANES), F32)],
        compiler_params=_cparams(4),
    )(qn, knb, z_f, fcum, fcum)


def _past_bias_kernel(nh, pt_ref, lf_ref, o_ref, carry, xpad):
    j = pl.program_id(1)

    @pl.when(j == 0)
    def _():
        carry[...] = jnp.zeros(carry.shape, F32)
        xpad[...] = jnp.zeros(xpad.shape, F32)

    xpad[0:nh, :] = lf_ref[...]
    x = xpad[...]
    after = (_iota((LANES, LANES), 0) > _iota((LANES, LANES), 1)).astype(F32)
    o_ref[...] = _dot(x, after, HI) + carry[...]
    carry[...] = carry[...] + _dot(x, jnp.ones((LANES, LANES), F32), HI)


def _past_bias(logf_t, layer, page_table):
    nh, page = logf_t.shape[2], logf_t.shape[3]
    nb, npg = page_table.shape
    grid_spec = pltpu.PrefetchScalarGridSpec(
        num_scalar_prefetch=1,
        grid=(nb, npg),
        in_specs=[pl.BlockSpec((None, None, nh, page), lambda b, j, pt: (layer, pt[b, npg - 1 - j], 0, 0))],
        out_specs=pl.BlockSpec((None, None, 16, page), lambda b, j, pt: (b, npg - 1 - j, 0, 0)),
        scratch_shapes=[pltpu.VMEM((16, page), F32), pltpu.VMEM((16, page), F32)],
    )
    return pl.pallas_call(
        functools.partial(_past_bias_kernel, nh),
        grid_spec=grid_spec,
        out_shape=jax.ShapeDtypeStruct((nb, npg, 16, page), F32),
        compiler_params=_cparams(2),
    )(page_table, logf_t)


def _decode_kernel(nh, pt_ref, q_ref, kn_ref, vn_ref, lfn_ref, bias_ref, k_ref, v_ref, o_ref,
                   m_ref, l_ref, acc_ref):
    p = pl.program_id(1)
    page = k_ref.shape[0]
    hd = k_ref.shape[2]

    @pl.when(p == 0)
    def _():
        m_ref[...] = jnp.full(m_ref.shape, NEG, F32)
        l_ref[...] = jnp.zeros(l_ref.shape, F32)
        acc_ref[...] = jnp.zeros(acc_ref.shape, F32)

    def update(h, s, vmat):
        m_prev = m_ref[h]
        m_new = jnp.maximum(m_prev, jnp.max(s, axis=-1, keepdims=True))
        alpha = jnp.exp(m_prev - m_new)
        pr = jnp.exp(s - m_new)
        l_ref[h] = alpha * l_ref[h] + jnp.sum(pr, axis=-1, keepdims=True)
        acc_ref[h] = alpha * acc_ref[h] + _dot(pr.astype(BF16), vmat)
        m_ref[h] = m_new

    for h in range(nh):
        q8 = jnp.broadcast_to(q_ref[h:h + 1, :], (SUBLANES, hd)).astype(BF16)
        kh = k_ref[:, h, :].astype(BF16)
        s = _dot_nt(q8, kh) + (bias_ref[h:h + 1, :] + lfn_ref[h:h + 1, :])
        update(h, s, v_ref[:, h, :].astype(BF16))

    @pl.when(p == pl.num_programs(1) - 1)
    def _():
        first = _iota((page, hd), 0) == 0
        lane_first = _iota((SUBLANES, page), 1) == 0
        for h in range(nh):
            q8 = jnp.broadcast_to(q_ref[h:h + 1, :], (SUBLANES, hd)).astype(BF16)
            kmat = jnp.where(first, kn_ref[h:h + 1, :], 0.0).astype(BF16)
            vmat = jnp.where(first, vn_ref[h:h + 1, :], 0.0).astype(BF16)
            s = jnp.where(lane_first, _dot_nt(q8, kmat), NEG)
            update(h, s, vmat)
            o_ref[h:h + 1, :] = (acc_ref[h] / l_ref[h])[0:1, :]


def _decode_attention(q, k_new, v_new, logf_new_b, bias, cache_k, cache_v, layer, page_table):
    nb, nh, hd = q.shape
    npg = page_table.shape[1]
    page = cache_k.shape[2]
    small = pl.BlockSpec((None, nh, hd), lambda b, p, pt: (b, 0, 0))
    cache = pl.BlockSpec((None, None, page, nh, hd), lambda b, p, pt: (layer, pt[b, p], 0, 0, 0))
    grid_spec = pltpu.PrefetchScalarGridSpec(
        num_scalar_prefetch=1,
        grid=(nb, npg),
        in_specs=[small, small, small,
                  pl.BlockSpec((None, 16, page), lambda b, p, pt: (b, 0, 0)),
                  pl.BlockSpec((None, None, 16, page), lambda b, p, pt: (b, p, 0, 0)),
                  cache, cache],
        out_specs=pl.BlockSpec((None, nh, hd), lambda b, p, pt: (b, 0, 0)),
        scratch_shapes=[pltpu.VMEM((nh, SUBLANES, 1), F32), pltpu.VMEM((nh, SUBLANES, 1), F32),
                        pltpu.VMEM((nh, SUBLANES, hd), F32)],
    )
    return pl.pallas_call(
        functools.partial(_decode_kernel, nh),
        grid_spec=grid_spec,
        out_shape=jax.ShapeDtypeStruct((nb, nh, hd), F32),
        compiler_params=_cparams(2),
    )(page_table, q, k_new, v_new, logf_new_b, bias, cache_k, cache_v)


def _merge_kernel(ya_ref, yb_ref, yc_ref, wa_ref, wb_ref, wc_ref, ga_ref, gb_ref, gc_ref, o_ref,
                  was, wbs, wcs):
    @pl.when(pl.program_id(1) == 0)
    def _():
        was[...] = wa_ref[...].astype(BF16)
        wbs[...] = wb_ref[...].astype(BF16)
        wcs[...] = wc_ref[...].astype(BF16)

    acc = _sigmoid(ga_ref[...]) * _dot(ya_ref[...], was[...])
    acc = acc + _sigmoid(gb_ref[...]) * _dot(yb_ref[...], wbs[...])
    acc = acc + _sigmoid(gc_ref[...]) * _dot(yc_ref[...], wcs[...])
    o_ref[...] = acc.astype(o_ref.dtype)


def _merge(ya, yb, yc, w_oa, w_ob, w_oc, layer, z_g):
    r = ya.shape[0]
    ka, kb, kc = ya.shape[1], yb.shape[1], yc.shape[1]
    d = w_oa.shape[2]
    tn = _tile(d, 512)
    nj = d // tn
    wspec = lambda k: pl.BlockSpec((None, k, tn), lambda j, i: (layer, 0, j))
    gspec = lambda o: pl.BlockSpec((TM, tn), lambda j, i: (i, o * nj + j))
    return pl.pallas_call(
        _merge_kernel,
        grid=(nj, r // TM),
        in_specs=[pl.BlockSpec((TM, ka), lambda j, i: (i, 0)),
                  pl.BlockSpec((TM, kb), lambda j, i: (i, 0)),
                  pl.BlockSpec((TM, kc), lambda j, i: (i, 0)),
                  wspec(ka), wspec(kb), wspec(kc), gspec(0), gspec(1), gspec(2)],
        out_specs=pl.BlockSpec((TM, tn), lambda j, i: (i, j)),
        out_shape=jax.ShapeDtypeStruct((r, d), BF16),
        scratch_shapes=[pltpu.VMEM((ka, tn), BF16), pltpu.VMEM((kb, tn), BF16), pltpu.VMEM((kc, tn), BF16)],
        compiler_params=_cparams(2),
    )(ya, yb, yc, w_oa, w_ob, w_oc, z_g, z_g, z_g)


def _residual_mm_kernel(m_ref, w_ref, x_ref, gt_ref, o_ref, ws):
    @pl.when(pl.program_id(1) == 0)
    def _():
        ws[...] = w_ref[...].astype(BF16)

    y = _dot(m_ref[...], ws[...])
    tm, tn = y.shape
    y3 = y.reshape(tm // SUBLANES, SUBLANES, tn) * gt_ref[...][None]
    o_ref[...] = x_ref[...] + y3.reshape(tm, tn)


def _residual_mm(m, w_o, layer, x, gt, tiles_per_seq):
    r, k = m.shape
    d = w_o.shape[2]
    tn = _tile(d, 512)
    n_prompt = gt.shape[0] - 1
    return pl.pallas_call(
        _residual_mm_kernel,
        grid=(d // tn, r // TM),
        in_specs=[pl.BlockSpec((TM, k), lambda j, i: (i, 0)),
                  pl.BlockSpec((None, k, tn), lambda j, i: (layer, 0, j)),
                  pl.BlockSpec((TM, tn), lambda j, i: (i, j)),
                  pl.BlockSpec((None, SUBLANES, tn),
                               lambda j, i: (jnp.minimum(i // tiles_per_seq, n_prompt), 0, j))],
        out_specs=pl.BlockSpec((TM, tn), lambda j, i: (i, j)),
        out_shape=jax.ShapeDtypeStruct((r, d), F32),
        scratch_shapes=[pltpu.VMEM((k, tn), BF16)],
        compiler_params=_cparams(2),
    )(m, w_o, x, gt)


GATHER_WINDOW = 256


def _gather_rows_kernel(tok_ref, n_ref, src_ref, dst_ref, sem):
    n_win = n_ref[0] // GATHER_WINDOW

    def copy(i):
        return pltpu.make_async_copy(src_ref.at[pl.ds(tok_ref[i], 1)], dst_ref.at[pl.ds(i, 1)], sem)

    def issue(wi):
        def body(r, carry):
            copy(wi * GATHER_WINDOW + r).start()
            return carry
        lax.fori_loop(0, GATHER_WINDOW, body, 0)

    def drain(wi):
        def body(r, carry):
            copy(wi * GATHER_WINDOW + r).wait()
            return carry
        lax.fori_loop(0, GATHER_WINDOW, body, 0)

    def window(wi, carry):
        issue(wi)

        @pl.when(wi > 0)
        def _():
            drain(wi - 1)
        return carry

    lax.fori_loop(0, n_win, window, 0)

    @pl.when(n_win > 0)
    def _():
        drain(n_win - 1)


def _gather_rows(src, tok, n_rows, n_out):
    grid_spec = pltpu.PrefetchScalarGridSpec(
        num_scalar_prefetch=2,
        grid=(1,),
        in_specs=[pl.BlockSpec(memory_space=pl.ANY)],
        out_specs=pl.BlockSpec(memory_space=pl.ANY),
        scratch_shapes=[pltpu.SemaphoreType.DMA(())],
    )
    return pl.pallas_call(
        _gather_rows_kernel,
        grid_spec=grid_spec,
        out_shape=jax.ShapeDtypeStruct((n_out, src.shape[1]), src.dtype),
        compiler_params=_cparams(1),
    )(tok, n_rows, src)


def _unpack(xp):
    lo = pltpu.bitcast(xp << 16, F32).astype(BF16)
    hi = pltpu.bitcast(xp & jnp.uint32(0xFFFF0000), F32).astype(BF16)
    return lo, hi


def _expert_up_kernel(te_ref, na_ref, x_ref, wg_ref, wu_ref, bg_ref, bu_ref, o_ref, wgs, wus):
    t = pl.program_id(1)
    fresh = jnp.logical_or(t == 0, te_ref[t] != te_ref[jnp.maximum(t - 1, 0)])

    @pl.when(fresh)
    def _():
        wgs[...] = wg_ref[...].astype(BF16)
        wus[...] = wu_ref[...].astype(BF16)

    @pl.when(t < na_ref[0])
    def _():
        lo, hi = _unpack(x_ref[...])
        half = lo.shape[1]
        g = _dot(lo, wgs[0:half, :]) + _dot(hi, wgs[half:, :]) + bg_ref[...]
        u = _dot(lo, wus[0:half, :]) + _dot(hi, wus[half:, :]) + bu_ref[...]
        g = jnp.minimum(g, SWIGLU_LIMIT)
        u = jnp.clip(u, -SWIGLU_LIMIT, SWIGLU_LIMIT)
        o_ref[...] = (g * _sigmoid(SWIGLU_ALPHA * g) * (u + 1.0)).astype(o_ref.dtype)

    @pl.when(t >= na_ref[0])
    def _():
        o_ref[...] = jnp.zeros(o_ref.shape, o_ref.dtype)


def _expert_up(xs, w_gate, w_up, b_gate, b_up, layer, tile_expert, n_active):
    p, half = xs.shape
    n_exp, d, f = w_gate.shape[1], w_gate.shape[2], w_gate.shape[3]
    tf = _tile(f, 512)
    n_tiles = p // TM
    xrow = lambda fc, t, te, na: (jnp.minimum(t, na[0] - 1), 0)
    wspec = pl.BlockSpec((None, None, d, tf), lambda fc, t, te, na: (layer, te[t], 0, fc))
    bspec = pl.BlockSpec((None, None, 1, tf), lambda fc, t, te, na: (layer, te[t], 0, fc))
    grid_spec = pltpu.PrefetchScalarGridSpec(
        num_scalar_prefetch=2,
        grid=(f // tf, n_tiles),
        in_specs=[pl.BlockSpec((TM, half), xrow), wspec, wspec, bspec, bspec],
        out_specs=pl.BlockSpec((TM, tf), lambda fc, t, te, na: (t, fc)),
        scratch_shapes=[pltpu.VMEM((d, tf), BF16), pltpu.VMEM((d, tf), BF16)],
    )
    n_layers = w_gate.shape[0]
    return pl.pallas_call(
        _expert_up_kernel,
        grid_spec=grid_spec,
        out_shape=jax.ShapeDtypeStruct((p, f), BF16),
        compiler_params=_cparams(2, 60 * 1024 * 1024),
    )(tile_expert, n_active, xs, w_gate, w_up,
      b_gate.reshape(n_layers, n_exp, 1, f), b_up.reshape(n_layers, n_exp, 1, f))


def _expert_down_kernel(te_ref, na_ref, h_ref, w_ref, b_ref, o_ref, ws):
    t = pl.program_id(1)
    fresh = jnp.logical_or(t == 0, te_ref[t] != te_ref[jnp.maximum(t - 1, 0)])

    @pl.when(fresh)
    def _():
        ws[...] = w_ref[...].astype(BF16)

    o_ref[...] = _dot(h_ref[...], ws[...]) + b_ref[...]


def _expert_down(hmid, w_down, b_down, layer, tile_expert, n_active):
    p, f = hmid.shape
    n_layers, n_exp, _, d = w_down.shape
    tn = _tile(d, 1024)
    grid_spec = pltpu.PrefetchScalarGridSpec(
        num_scalar_prefetch=2,
        grid=(d // tn, p // TM),
        in_specs=[pl.BlockSpec((TM, f), lambda j, t, te, na: (t, 0)),
                  pl.BlockSpec((None, None, f, tn), lambda j, t, te, na: (layer, te[t], 0, j)),
                  pl.BlockSpec((None, None, 1, tn), lambda j, t, te, na: (layer, te[t], 0, j))],
        out_specs=pl.BlockSpec((TM, tn), lambda j, t, te, na: (t, j)),
        scratch_shapes=[pltpu.VMEM((f, tn), BF16)],
    )
    return pl.pallas_call(
        _expert_down_kernel,
        grid_spec=grid_spec,
        out_shape=jax.ShapeDtypeStruct((p, d), F32),
        compiler_params=_cparams(2),
    )(tile_expert, n_active, hmid, w_down, b_down.reshape(n_layers, n_exp, 1, d))


COMBINE_ROWS = 64


def _combine_kernel(pos_ref, y_ref, x_ref, wt_ref, gt_ref, o_ref, buf, sem):
    i = pl.program_id(0)
    tm = x_ref.shape[0]

    def copy(r, k):
        src = pos_ref[(i * tm + r) * TOP_K + k]
        return pltpu.make_async_copy(y_ref.at[pl.ds(src, 1)], buf.at[k, pl.ds(r, 1)], sem)

    def issue(r, carry):
        for k in range(TOP_K):
            copy(r, k).start()
        return carry

    def drain(r, carry):
        for k in range(TOP_K):
            copy(r, k).wait()
        return carry

    lax.fori_loop(0, tm, issue, 0)
    lax.fori_loop(0, tm, drain, 0)
    wt = wt_ref[...]
    acc = wt[:, 0:1] * buf[0]
    for k in range(1, TOP_K):
        acc = acc + wt[:, k:k + 1] * buf[k]
    d = acc.shape[1]
    a3 = acc.reshape(tm // SUBLANES, SUBLANES, d) * gt_ref[...][None]
    o_ref[...] = x_ref[...] + a3.reshape(tm, d)


def _combine(y, pos, x, wts, gt, rows_per_seq):
    r, d = x.shape
    tm = COMBINE_ROWS
    n_prompt = gt.shape[0] - 1
    tiles_per_seq = rows_per_seq // tm
    grid_spec = pltpu.PrefetchScalarGridSpec(
        num_scalar_prefetch=1,
        grid=(r // tm,),
        in_specs=[pl.BlockSpec(memory_space=pl.ANY),
                  pl.BlockSpec((tm, d), lambda i, pos: (i, 0)),
                  pl.BlockSpec((tm, LANES), lambda i, pos: (i, 0)),
                  pl.BlockSpec((None, SUBLANES, d),
                               lambda i, pos: (jnp.minimum(i // tiles_per_seq, n_prompt), 0, 0))],
        out_specs=pl.BlockSpec((tm, d), lambda i, pos: (i, 0)),
        scratch_shapes=[pltpu.VMEM((TOP_K, tm, d), F32), pltpu.SemaphoreType.DMA(())],
    )
    return pl.pallas_call(
        _combine_kernel,
        grid_spec=grid_spec,
        out_shape=jax.ShapeDtypeStruct((r, d), F32),
        compiler_params=_cparams(1),
    )(pos, y, x, wts, gt)


def _dispatch_plan(idx, n_tokens, n_experts, n_rows_max):
    eid = idx[:n_tokens, :TOP_K].reshape(-1)
    n_flat = eid.shape[0]
    order = jnp.argsort(eid, stable=True).astype(I32)
    sorted_e = eid[order]
    counts = jnp.zeros((n_experts,), I32).at[eid].add(1)
    padded = ((counts + TM - 1) // TM) * TM
    ends_p = jnp.cumsum(padded)
    starts_p = ends_p - padded
    starts = jnp.cumsum(counts) - counts
    pos_sorted = starts_p[sorted_e] + (jnp.arange(n_flat, dtype=I32) - starts[sorted_e])
    row_token = jnp.zeros((n_rows_max,), I32).at[pos_sorted].set(order // TOP_K)
    pos_flat = jnp.zeros((n_flat,), I32).at[order].set(pos_sorted)
    n_active = (ends_p[-1] // TM).astype(I32)
    tile_start = jnp.arange(n_rows_max // TM, dtype=I32) * TM
    tile_expert = jnp.minimum(jnp.searchsorted(ends_p, tile_start, side='right'), n_experts - 1).astype(I32)
    last_e = tile_expert[jnp.maximum(n_active - 1, 0)]
    tile_expert = jnp.where(tile_start < ends_p[-1], tile_expert, last_e)
    return row_token, pos_flat, tile_expert, n_active.reshape(1)


def _moe(x, h2p, idx, wts, gt, w_gate, b_gate, w_up, b_up, w_down, b_down, layer, n_tokens, rows_per_seq):
    r = x.shape[0]
    n_experts = w_gate.shape[1]
    n_rows_max = -(-(n_tokens * TOP_K + n_experts * (TM - 1)) // TM) * TM
    row_token, pos_flat, tile_expert, n_active = _dispatch_plan(idx, n_tokens, n_experts, n_rows_max)
    xs = _gather_rows(h2p, row_token, n_active * TM, n_rows_max)
    hmid = _expert_up(xs, w_gate, w_up, b_gate, b_up, layer, tile_expert, n_active)
    y = _expert_down(hmid, w_down, b_down, layer, tile_expert, n_active)
    pos = jnp.zeros((r * TOP_K,), I32).at[:n_tokens * TOP_K].set(pos_flat)
    lane = jnp.arange(LANES)[None, :]
    wts = jnp.where((jnp.arange(r)[:, None] < n_tokens) & (lane < TOP_K), wts, 0.0)
    return _combine(y, pos, x, wts, gt, rows_per_seq)


def kernel(x_prompt, x_sample, c_prompt, c_sample, cache_k, cache_v, cache_logf, page_table, state_delta, state_delta_conv, state_sconv, w_ada, b_ada, g_mix, w_in, w_sconv, w_dconv, a_log, dt_bias, g_dnorm, g_qnorm, g_knorm, b_forget, w_oa, w_ob, w_oc, w_o, g_ffn, w_router, b_router, w_gate, b_gate, w_up, b_up, w_down, b_down):
    bp, seq, d = x_prompt.shape
    nb = x_sample.shape[0]
    assert nb == SUBLANES and x_sample.shape[1] == 1 and bp + nb <= 16
    n_layers = w_ada.shape[0]
    cw = w_sconv.shape[2]
    nh_d, dk, dv = state_delta.shape[2], state_delta.shape[3], state_delta.shape[4]
    nh_f, hd = cache_k.shape[3], cache_k.shape[4]
    page = cache_k.shape[2]
    assert dk == LANES and dv == LANES and hd == LANES and page == LANES
    assert 2 * nh_d + nh_f <= LANES
    rp = bp * seq
    r = rp + TM
    tiles_per_seq = seq // TM
    n_tokens = rp + nb
    qk_d, v_d = nh_d * dk, nh_d * dv
    f_d = nh_f * hd
    tq = 512 if seq % 512 == 0 else TM

    sizes = (cw, cw, cw, qk_d, qk_d, v_d, v_d, nh_d, nh_d, f_d, f_d, f_d, nh_f, d, d, d)
    off = np.concatenate([[0], np.cumsum(sizes)]).tolist()
    dz_col0 = off[6]

    x = jnp.concatenate([x_prompt.reshape(rp, d), x_sample.reshape(nb, d), jnp.zeros((TM - nb, d), F32)], axis=0)
    c_rows = jnp.concatenate([c_prompt, c_sample, jnp.zeros((16 - bp - nb, d), F32)], axis=0)
    mod_all = _ada(c_rows, w_ada, b_ada)
    logf_t = jnp.transpose(cache_logf, (0, 1, 3, 2))
    zero_state = jnp.zeros((bp, nh_d, dk, dv), F32)

    def with_sample(prompt_rows, sample_rows):
        pad = jnp.zeros((TM - nb, prompt_rows.shape[1]), prompt_rows.dtype)
        return jnp.concatenate([prompt_rows, sample_rows.astype(prompt_rows.dtype), pad], axis=0)

    def to_chunks(rows):
        return jnp.pad(rows[:, None, :], ((0, 0), (0, DELTA_CHUNK - 1), (0, 0))).reshape(nb * DELTA_CHUNK, -1)

    outs_p, outs_s = [], []
    for l in range(n_layers):
        mod = mod_all[l]
        modx = jnp.concatenate([jnp.broadcast_to(mod[:bp, None, :], (bp, SUBLANES, 6 * d)),
                                mod[None, bp:bp + nb, :]], axis=0)
        sh_m, sc_m, gt_m, sh_f, sc_f, gt_f = jnp.split(modx, 6, axis=-1)

        h = _norm(x, g_mix[l], sc_m, sh_m, tiles_per_seq)
        w = w_in[l]
        w_a = w[:, :off[7]].astype(BF16)
        w_s = jnp.concatenate([w[:, off[7]:off[9]], w[:, off[12]:off[13]],
                               jnp.zeros((d, LANES - 2 * nh_d - nh_f), F32)], axis=1).astype(BF16)
        w_f = w[:, off[9]:off[12]].astype(BF16)
        w_g = w[:, off[13]:].astype(BF16)
        z_a = _mm(h, w_a)
        z_f = _mm(h, w_f)
        z_g = _mm(h, w_g)
        z_s = _mm(h, w_s)
        act = _gate_acts(z_s, a_log[l], dt_bias[l], b_forget[l])
        z_a_s = z_a[rp:rp + nb]
        act_s = act[rp:rp + nb]

        ya_p, sconv_p = _conv_prompt(z_a, w_sconv[l], 0, cw, bp, seq, True, BF16)
        ya_s, sconv_s = _conv_step(z_a_s, w_sconv[l], state_sconv[l], 0, cw, True, BF16)
        taps_a = w_sconv.shape[1]

        qkv_p, dconv_p = _conv_prompt(z_a, w_dconv[l], off[3], 2 * qk_d + v_d, bp, seq, False, F32)
        qkv_s, dconv_s = _conv_step(z_a_s, w_dconv[l], state_delta_conv[l], off[3], 2 * qk_d + v_d, False, F32)
        taps_b = w_dconv.shape[1]
        yb_p, delta_p = _delta(qkv_p, act, z_a, dz_col0, zero_state, g_dnorm[l], bp, seq)
        act_s_chunks = to_chunks(jnp.where(jnp.arange(LANES)[None, :] < 2 * nh_d, act_s, 0.0))
        yb_s_chunks, delta_s = _delta(to_chunks(qkv_s), act_s_chunks, to_chunks(z_a_s[:, dz_col0:dz_col0 + v_d]), 0,
                                      state_delta[l], g_dnorm[l], nb, DELTA_CHUNK)
        yb_s = yb_s_chunks[::DELTA_CHUNK]

        qn, kn, knb, fcum = _qkprep(z_f, act, g_qnorm[l], g_knorm[l], nh_f, tiles_per_seq)
        yc_p = _flash(qn, knb, z_f, fcum, nh_f, 2 * nh_d, bp, seq, tq)
        q_s = qn[rp:rp + nb].astype(F32).reshape(nb, nh_f, hd)
        k_s = kn[rp:rp + nb].reshape(nb, nh_f, hd)
        v_s = z_f[rp:rp + nb, 2 * f_d:].reshape(nb, nh_f, hd)
        logf_s = act_s[:, 2 * nh_d:2 * nh_d + nh_f]
        logf_s_b = jnp.broadcast_to(jnp.pad(logf_s, ((0, 0), (0, 16 - nh_f)))[:, :, None], (nb, 16, page))
        bias = _past_bias(logf_t, l, page_table)
        yc_s = _decode_attention(q_s, k_s, v_s, logf_s_b, bias, cache_k, cache_v, l, page_table).reshape(nb, f_d)

        ya = with_sample(ya_p, ya_s)
        yb = with_sample(yb_p, yb_s)
        yc = with_sample(yc_p, yc_s)
        merged = _merge(ya, yb, yc, w_oa, w_ob, w_oc, l, z_g)
        x = _residual_mm(merged, w_o, l, x, gt_m, tiles_per_seq)

        h2p, idx, wts = _norm_router(x, g_ffn[l], sc_f, sh_f, w_router[l], b_router[l], tiles_per_seq)
        x = _moe(x, h2p, idx, wts, gt_f, w_gate, b_gate, w_up, b_up, w_down, b_down, l, n_tokens, seq)

        outs_p.append((kn[:rp].reshape(bp, seq, nh_f, hd),
                       z_f[:rp, 2 * f_d:].reshape(bp, seq, nh_f, hd),
                       act[:rp, 2 * nh_d:2 * nh_d + nh_f].reshape(bp, seq, nh_f),
                       delta_p,
                       dconv_p[:, SUBLANES - (taps_b - 1):],
                       sconv_p[:, SUBLANES - (taps_a - 1):]))
        outs_s.append((k_s.reshape(nb, 1, nh_f, hd), v_s.reshape(nb, 1, nh_f, hd), logf_s.reshape(nb, 1, nh_f),
                       delta_s, dconv_s, sconv_s))

    def stack(sts, i):
        return jnp.stack([st[i] for st in sts], axis=0)

    return (x[:rp].reshape(bp, seq, d), x[rp:rp + nb].reshape(nb, 1, d),
            stack(outs_p, 0), stack(outs_p, 1), stack(outs_p, 2), stack(outs_p, 3), stack(outs_p, 4), stack(outs_p, 5),
            stack(outs_s, 0), stack(outs_s, 1), stack(outs_s, 2), stack(outs_s, 3), stack(outs_s, 4), stack(outs_s, 5))
```

```python
import functools

import jax
import jax.numpy as jnp
import numpy as np
from jax import lax
from jax.experimental import pallas as pl
from jax.experimental.pallas import tpu as pltpu

F32 = jnp.float32
BF16 = jnp.bfloat16
I32 = jnp.int32
U32 = jnp.uint32
HI = lax.Precision.HIGHEST

EPS = 1e-6
TOP_K = 4
DELTA_CHUNK = 64
SWIGLU_ALPHA = 1.702
SWIGLU_LIMIT = 7.0
LANES = 128
SUBLANES = 8
HEAD_ROWS = 16
TM = 256
NEG = -1e30
VMEM_LIMIT = 56 * 1024 * 1024


def _cparams(n_axes, vmem=None, unchecked=False):
    return pltpu.CompilerParams(dimension_semantics=("arbitrary",) * n_axes,
                                vmem_limit_bytes=vmem or VMEM_LIMIT,
                                disable_bounds_checks=unchecked)


def _dot(a, b, precision=None):
    return jnp.dot(a, b, preferred_element_type=F32, precision=precision)


def _dot_nt(a, b, precision=None):
    return lax.dot_general(a, b, (((1,), (1,)), ((), ())), preferred_element_type=F32, precision=precision)


def _dot_tn(a, b, precision=None):
    return lax.dot_general(a, b, (((0,), (0,)), ((), ())), preferred_element_type=F32, precision=precision)


def _bdot(a, b):
    return lax.dot_general(a, b, (((2,), (1,)), ((0,), (0,))), preferred_element_type=F32)


def _bdot_nt(a, b):
    return lax.dot_general(a, b, (((2,), (2,)), ((0,), (0,))), preferred_element_type=F32)


def _split3(x):
    p1 = x.astype(BF16)
    r1 = x - p1.astype(F32)
    p2 = r1.astype(BF16)
    p3 = (r1 - p2.astype(F32)).astype(BF16)
    return p1, p2, p3


def _split2(x):
    hi = x.astype(BF16)
    return hi, (x - hi.astype(F32)).astype(BF16)


def _select_dot(x, sel, nt=False):
    d = _dot_nt if nt else _dot
    if nt:
        return sum(d(sel, p) for p in _split3(x))
    return sum(d(p, sel) for p in _split3(x))


def _dot3(a_hi, a_lo, b_hi, b_lo):
    return _dot(a_hi, b_hi) + (_dot(a_hi, b_lo) + _dot(a_lo, b_hi))


def _tile(n, preferred):
    t = preferred
    while n % t:
        t //= 2
    assert t >= LANES, (n, preferred)
    return t


def _iota(shape, dim):
    return lax.broadcasted_iota(I32, shape, dim)


def _sigmoid(x):
    return 1.0 / (1.0 + jnp.exp(-x))


def _ada_kernel(c_ref, w_ref, b_ref, o_ref):
    c = c_ref[...]
    a = (c * _sigmoid(c)).astype(BF16)
    o_ref[...] = _dot(a, w_ref[...].astype(BF16)) + b_ref[...]


def _ada(c_rows, w_ada, b_ada):
    n_layers, d, n = w_ada.shape
    tn = _tile(n, 512)
    return pl.pallas_call(
        _ada_kernel,
        grid=(n_layers, n // tn),
        in_specs=[pl.BlockSpec((16, d), lambda l, j: (0, 0)),
                  pl.BlockSpec((None, d, tn), lambda l, j: (l, 0, j)),
                  pl.BlockSpec((None, 1, tn), lambda l, j: (l, 0, j))],
        out_specs=pl.BlockSpec((None, 16, tn), lambda l, j: (l, 0, j)),
        out_shape=jax.ShapeDtypeStruct((n_layers, 16, n), F32),
        compiler_params=_cparams(2),
    )(c_rows, w_ada, b_ada.reshape(n_layers, 1, n))


def _modulated_norm(x, g, sc, sh):
    ms = jnp.mean(x * x, axis=-1, keepdims=True)
    y = x * lax.rsqrt(ms + EPS) * g
    tm, d = y.shape
    y3 = y.reshape(tm // SUBLANES, SUBLANES, d)
    return (y3 * (1.0 + sc[None]) + sh[None]).reshape(tm, d)


def _norm_kernel(x_ref, g_ref, sc_ref, sh_ref, o_ref):
    o_ref[...] = _modulated_norm(x_ref[...], g_ref[...], sc_ref[...], sh_ref[...]).astype(o_ref.dtype)


def _mod_index(tiles_per_seq, n_prompt):
    return lambda i: (jnp.minimum(i // tiles_per_seq, n_prompt), 0, 0)


def _norm(x, g, sc, sh, tiles_per_seq):
    r, d = x.shape
    n_prompt = sc.shape[0] - 1
    mod_spec = pl.BlockSpec((None, SUBLANES, d), _mod_index(tiles_per_seq, n_prompt))
    return pl.pallas_call(
        _norm_kernel,
        grid=(r // TM,),
        in_specs=[pl.BlockSpec((TM, d), lambda i: (i, 0)),
                  pl.BlockSpec((1, d), lambda i: (0, 0)),
                  mod_spec, mod_spec],
        out_specs=pl.BlockSpec((TM, d), lambda i: (i, 0)),
        out_shape=jax.ShapeDtypeStruct((r, d), BF16),
        compiler_params=_cparams(1),
    )(x, g.reshape(1, d), sc, sh)


def _norm_router_kernel(n_experts, x_ref, g_ref, sc_ref, sh_ref, wr_ref, br_ref, hp_ref, idx_ref, wt_ref):
    h = _modulated_norm(x_ref[...], g_ref[...], sc_ref[...], sh_ref[...])
    tm, d = h.shape
    lo = pltpu.bitcast(h[:, :d // 2].astype(BF16).astype(F32), U32)
    hi = pltpu.bitcast(h[:, d // 2:].astype(BF16).astype(F32), U32)
    hp_ref[...] = (hi & jnp.uint32(0xFFFF0000)) | (lo >> 16)
    logits = _dot(h, wr_ref[...], HI) + br_ref[...]
    lane = _iota(logits.shape, 1)
    work = jnp.where(lane < n_experts, logits, NEG)
    vals, idxs = [], []
    for _ in range(TOP_K):
        m = jnp.max(work, axis=-1, keepdims=True)
        ix = jnp.min(jnp.where(work == m, lane, LANES), axis=-1, keepdims=True)
        vals.append(m)
        idxs.append(ix)
        work = jnp.where(lane == ix, NEG, work)
    es = [jnp.exp(v - vals[0]) for v in vals]
    tot = es[0] + es[1] + es[2] + es[3]
    idx_out = jnp.zeros(logits.shape, I32)
    wt_out = jnp.zeros(logits.shape, F32)
    for k in range(TOP_K):
        idx_out = jnp.where(lane == k, idxs[k], idx_out)
        wt_out = jnp.where(lane == k, es[k] / tot, wt_out)
    idx_ref[...] = idx_out
    wt_ref[...] = wt_out


def _norm_router(x, g, sc, sh, w_router, b_router, tiles_per_seq):
    r, d = x.shape
    n_experts = w_router.shape[1]
    n_prompt = sc.shape[0] - 1
    wr = jnp.pad(w_router, ((0, 0), (0, LANES - n_experts)))
    br = jnp.pad(b_router, (0, LANES - n_experts)).reshape(1, LANES)
    mod_spec = pl.BlockSpec((None, SUBLANES, d), _mod_index(tiles_per_seq, n_prompt))
    return pl.pallas_call(
        functools.partial(_norm_router_kernel, n_experts),
        grid=(r // TM,),
        in_specs=[pl.BlockSpec((TM, d), lambda i: (i, 0)),
                  pl.BlockSpec((1, d), lambda i: (0, 0)),
                  mod_spec, mod_spec,
                  pl.BlockSpec((d, LANES), lambda i: (0, 0)),
                  pl.BlockSpec((1, LANES), lambda i: (0, 0))],
        out_specs=[pl.BlockSpec((TM, d // 2), lambda i: (i, 0)),
                   pl.BlockSpec((TM, LANES), lambda i: (i, 0)),
                   pl.BlockSpec((TM, LANES), lambda i: (i, 0))],
        out_shape=[jax.ShapeDtypeStruct((r, d // 2), U32),
                   jax.ShapeDtypeStruct((r, LANES), I32),
                   jax.ShapeDtypeStruct((r, LANES), F32)],
        compiler_params=_cparams(1),
    )(x, g.reshape(1, d), sc, sh, wr, br)


XPOSE = 512


def _transpose_to_bf16(src, dst):
    for kc in range(dst.shape[0] // XPOSE):
        dst[kc * XPOSE:(kc + 1) * XPOSE, :] = src[:, kc * XPOSE:(kc + 1) * XPOSE].T.astype(BF16)


def _in_proj_kernel(layer, row0, squash, h_ref, wt_ref, o_ref, wbuf, ws, sem):
    j = pl.program_id(0)
    i = pl.program_id(1)
    tn = ws.shape[1]

    def copy(tile, slot):
        return pltpu.make_async_copy(wt_ref.at[pl.ds(row0 + tile * tn, tn), layer], wbuf.at[slot], sem.at[slot])

    @pl.when(i == 0)
    def _():
        @pl.when(j == 0)
        def _():
            copy(0, 0).start()

        @pl.when(j + 1 < pl.num_programs(0))
        def _():
            copy(j + 1, (j + 1) % 2).start()

        slot = j % 2
        copy(j, slot).wait()
        _transpose_to_bf16(wbuf.at[slot], ws)

    y = _dot(h_ref[...], ws[...])
    if squash:
        y = _sigmoid(y)
    o_ref[...] = y.astype(o_ref.dtype)


def _in_proj(h, wt, layer, col0, width, squash, out_dtype):
    r, d = h.shape
    tn = _tile(width, 1024)
    assert d % XPOSE == 0 or d == XPOSE
    return pl.pallas_call(
        functools.partial(_in_proj_kernel, layer, col0, squash),
        grid=(width // tn, r // TM),
        in_specs=[pl.BlockSpec((TM, d), lambda j, i: (i, 0)),
                  pl.BlockSpec(memory_space=pl.ANY)],
        out_specs=pl.BlockSpec((TM, tn), lambda j, i: (i, j)),
        out_shape=jax.ShapeDtypeStruct((r, width), out_dtype),
        scratch_shapes=[pltpu.VMEM((2, tn, d), F32), pltpu.VMEM((d, tn), BF16), pltpu.SemaphoreType.DMA((2,))],
        compiler_params=_cparams(2),
    )(h, wt)


def _in_small_kernel(layer, row_a, row_b, nh_d, nh_f, h_ref, wt_ref, p_ref, o_ref, wbuf, was, wbs, sem):
    @pl.when(pl.program_id(0) == 0)
    def _():
        ca = pltpu.make_async_copy(wt_ref.at[pl.ds(row_a, LANES), layer], wbuf.at[0], sem.at[0])
        cb = pltpu.make_async_copy(wt_ref.at[pl.ds(row_b, LANES), layer], wbuf.at[1], sem.at[1])
        ca.start()
        cb.start()
        ca.wait()
        cb.wait()
        _transpose_to_bf16(wbuf.at[0], was)
        _transpose_to_bf16(wbuf.at[1], wbs)

    h = h_ref[...]
    lane = _iota((h.shape[0], LANES), 1)
    z = jnp.where(lane < 2 * nh_d, _dot(h, was[...]), _dot(h, wbs[...]))
    beta = _sigmoid(z)
    za = z + p_ref[1:2, :]
    softplus = jnp.maximum(za, 0.0) + jnp.log1p(jnp.exp(-jnp.abs(za)))
    g = -jnp.exp(p_ref[0:1, :]) * softplus
    zf = z + p_ref[2:3, :]
    logf = jnp.minimum(zf, 0.0) - jnp.log1p(jnp.exp(-jnp.abs(zf)))
    o_ref[...] = jnp.where(lane < nh_d, beta, jnp.where(lane < 2 * nh_d, g, logf))


def _in_small(h, wt, layer, col_ba, col_f, a_log, dt_bias, b_forget):
    r, d = h.shape
    nh_d, nh_f = a_log.shape[0], b_forget.shape[0]
    row_b = col_f - 2 * nh_d
    assert col_ba + LANES <= wt.shape[0] and row_b >= 0 and row_b + LANES <= wt.shape[0]
    p = jnp.zeros((SUBLANES, LANES), F32)
    p = p.at[0, nh_d:2 * nh_d].set(a_log).at[1, nh_d:2 * nh_d].set(dt_bias)
    p = p.at[2, 2 * nh_d:2 * nh_d + nh_f].set(b_forget)
    return pl.pallas_call(
        functools.partial(_in_small_kernel, layer, col_ba, row_b, nh_d, nh_f),
        grid=(r // TM,),
        in_specs=[pl.BlockSpec((TM, d), lambda i: (i, 0)),
                  pl.BlockSpec(memory_space=pl.ANY),
                  pl.BlockSpec((SUBLANES, LANES), lambda i: (0, 0))],
        out_specs=pl.BlockSpec((TM, LANES), lambda i: (i, 0)),
        out_shape=jax.ShapeDtypeStruct((r, LANES), F32),
        scratch_shapes=[pltpu.VMEM((2, LANES, d), F32), pltpu.VMEM((d, LANES), BF16),
                        pltpu.VMEM((d, LANES), BF16), pltpu.SemaphoreType.DMA((2,))],
        compiler_params=_cparams(1),
    )(h, wt, p)


def _conv_prompt_kernel(taps, gated, *refs):
    if gated:
        ab_ref, ac_ref, ah_ref, w_ref, y_ref, st_ref, ubuf = refs
    else:
        x_ref, w_ref, y_ref, st_ref, ubuf = refs
    s = pl.program_id(2)
    ts = y_ref.shape[0]

    @pl.when(s == 0)
    def _():
        ubuf[0:SUBLANES, :] = jnp.zeros((SUBLANES, ubuf.shape[1]), F32)

    u = ac_ref[...] * ah_ref[...] if gated else x_ref[...]
    ubuf[SUBLANES:SUBLANES + ts, :] = u
    base = SUBLANES - (taps - 1)
    acc = w_ref[0:1, :] * ubuf[base:base + ts, :]
    for i in range(1, taps):
        acc = acc + w_ref[i:i + 1, :] * ubuf[base + i:base + i + ts, :]
    y = ab_ref[...] * acc if gated else acc * _sigmoid(acc)
    y_ref[...] = y.astype(y_ref.dtype)
    tail = ubuf[ts:ts + SUBLANES, :]
    ubuf[0:SUBLANES, :] = tail

    @pl.when(s == pl.num_programs(2) - 1)
    def _():
        st_ref[...] = tail


def _conv_prompt(z, w, col0, width, n_batch, seq, gated, out_dtype):
    taps = w.shape[0]
    tc = _tile(np.gcd(width, col0) if col0 else width, 512)
    ts = TM
    nt = seq // ts
    grid = (width // tc, n_batch, nt)
    row = lambda j, b, s: b * nt + s
    if gated:
        ins = [pl.BlockSpec((ts, tc), lambda j, b, s, o=o: (row(j, b, s), (col0 + o * width) // tc + j))
               for o in range(3)]
        args = (z, z, z, w)
    else:
        ins = [pl.BlockSpec((ts, tc), lambda j, b, s: (row(j, b, s), col0 // tc + j))]
        args = (z, w)
    ins.append(pl.BlockSpec((taps, tc), lambda j, b, s: (0, j)))
    return pl.pallas_call(
        functools.partial(_conv_prompt_kernel, taps, gated),
        grid=grid,
        in_specs=ins,
        out_specs=[pl.BlockSpec((ts, tc), lambda j, b, s: (row(j, b, s), j)),
                   pl.BlockSpec((None, SUBLANES, tc), lambda j, b, s: (b, 0, j))],
        out_shape=[jax.ShapeDtypeStruct((n_batch * seq, width), out_dtype),
                   jax.ShapeDtypeStruct((n_batch, SUBLANES, width), F32)],
        scratch_shapes=[pltpu.VMEM((ts + SUBLANES, tc), F32)],
        compiler_params=_cparams(3),
    )(*args)


def _conv_step_kernel(taps, gated, *refs):
    if gated:
        ab_ref, ac_ref, ah_ref, w_ref, buf_ref, y_ref, nb_ref = refs
        u = ac_ref[...] * ah_ref[...]
    else:
        x_ref, w_ref, buf_ref, y_ref, nb_ref = refs
        u = x_ref[...]
    acc = w_ref[taps - 1:taps, :] * u
    for i in range(taps - 1):
        acc = acc + w_ref[i:i + 1, :] * buf_ref[i]
    y = ab_ref[...] * acc if gated else acc * _sigmoid(acc)
    y_ref[...] = y.astype(y_ref.dtype)
    for i in range(taps - 2):
        nb_ref[i] = buf_ref[i + 1]
    nb_ref[taps - 2] = u


def _conv_step(z_rows, w, buf, col0, width, gated, out_dtype):
    taps = w.shape[0]
    tc = _tile(np.gcd(width, col0) if col0 else width, 512)
    nb = z_rows.shape[0]
    buf_t = jnp.transpose(buf, (1, 0, 2))
    if gated:
        ins = [pl.BlockSpec((nb, tc), lambda j, o=o: (0, (col0 + o * width) // tc + j)) for o in range(3)]
        args = (z_rows, z_rows, z_rows, w, buf_t)
    else:
        ins = [pl.BlockSpec((nb, tc), lambda j: (0, col0 // tc + j))]
        args = (z_rows, w, buf_t)
    ins += [pl.BlockSpec((taps, tc), lambda j: (0, j)),
            pl.BlockSpec((taps - 1, nb, tc), lambda j: (0, 0, j))]
    y, nbuf = pl.pallas_call(
        functools.partial(_conv_step_kernel, taps, gated),
        grid=(width // tc,),
        in_specs=ins,
        out_specs=[pl.BlockSpec((nb, tc), lambda j: (0, j)),
                   pl.BlockSpec((taps - 1, nb, tc), lambda j: (0, 0, j))],
        out_shape=[jax.ShapeDtypeStruct((nb, width), out_dtype),
                   jax.ShapeDtypeStruct((taps - 1, nb, width), F32)],
        compiler_params=_cparams(1),
    )(*args)
    return y, jnp.transpose(nbuf, (1, 0, 2))


def _delta_prep_kernel(nh, ch, act_ref, bb_ref, cb_ref):
    act = act_ref[...]
    tm = act.shape[0]
    ii = _iota((tm, tm), 0)
    jj = _iota((tm, tm), 1)
    tri = jnp.logical_and(ii >= jj, ii // ch == jj // ch).astype(BF16)
    cum = sum(_dot(tri, p) for p in _split3(act))
    for h in range(nh):
        bb_ref[h] = jnp.broadcast_to(act[:, h:h + 1], (tm, LANES))
        cb_ref[h] = jnp.broadcast_to(cum[:, nh + h:nh + h + 1], (tm, LANES))


def _delta_prep(act, nh):
    r = act.shape[0]
    return pl.pallas_call(
        functools.partial(_delta_prep_kernel, nh, DELTA_CHUNK),
        grid=(r // TM,),
        in_specs=[pl.BlockSpec((TM, LANES), lambda i: (i, 0))],
        out_specs=[pl.BlockSpec((nh, TM, LANES), lambda i: (0, i, 0)),
                   pl.BlockSpec((nh, TM, LANES), lambda i: (0, i, 0))],
        out_shape=[jax.ShapeDtypeStruct((nh, r, LANES), F32),
                   jax.ShapeDtypeStruct((nh, r, LANES), F32)],
        compiler_params=_cparams(1),
    )(act)


def _delta_kernel(hb, q_ref, k_ref, v_ref, bb_ref, cb_ref, dz_ref, s0_ref, gn_ref, o_ref, sf_ref, state):
    c = pl.program_id(2)
    ch = q_ref.shape[0]
    dk = LANES

    @pl.when(c == 0)
    def _():
        state[...] = s0_ref[...]

    def heads(ref):
        return jnp.stack([ref[:, i * dk:(i + 1) * dk] for i in range(hb)], axis=0)

    ii = _iota((ch, LANES), 0)[None]
    jj = _iota((ch, LANES), 1)[None]
    left = jj < ch
    pick0 = jnp.broadcast_to((jj == 0).astype(BF16), (hb, ch, LANES))
    eye_right = (jj - ch == ii).astype(F32)
    zrows = jnp.zeros((hb, ch, LANES), F32)
    zrows_b = jnp.zeros((hb, ch, LANES), BF16)
    n_levels = int(np.log2(ch))

    q = heads(q_ref)
    k = heads(k_ref)
    q = q * lax.rsqrt(jnp.sum(q * q, axis=-1, keepdims=True) + EPS) * (dk ** -0.5)
    k = k * lax.rsqrt(jnp.sum(k * k, axis=-1, keepdims=True) + EPS)
    beta_b = bb_ref[...]
    cum_b = cb_ref[...]
    cum_r = sum(_bdot_nt(pick0, p) for p in _split3(jnp.concatenate([cum_b, zrows], axis=1)))
    diff = cum_b - cum_r
    decay_strict = jnp.exp(jnp.where(jnp.logical_and(left, ii > jj), diff, NEG))
    decay_incl = jnp.exp(jnp.where(jnp.logical_and(left, ii >= jj), diff, NEG))
    kb = k * beta_b
    k_pad = jnp.concatenate([k, zrows], axis=1).astype(BF16)
    la = _bdot_nt(jnp.concatenate([kb, q], axis=1).astype(BF16), k_pad)
    lmat = la[:, :ch] * decay_strict
    attn = la[:, ch:] * decay_incl
    w = jnp.where(left, -lmat, eye_right)
    for _ in range(n_levels):
        w_hi, w_lo = _split2(w)
        p_hi = jnp.where(left, w_hi, jnp.zeros_like(w_hi))
        p_lo = jnp.where(left, w_lo, jnp.zeros_like(w_lo))
        r_hi = jnp.concatenate([w_hi, zrows_b], axis=1)
        r_lo = jnp.concatenate([w_lo, zrows_b], axis=1)
        r = _bdot(p_hi, r_hi) + (_bdot(p_hi, r_lo) + _bdot(p_lo, r_hi))
        w = jnp.where(left, r, w + r)
    t_right = jnp.where(left, 0.0, w).astype(BF16)
    e_cum = jnp.exp(cum_b)
    rhs = jnp.concatenate([heads(v_ref) * beta_b, kb * e_cum], axis=2)
    uw = _bdot(t_right, jnp.concatenate([jnp.zeros_like(rhs), rhs], axis=1).astype(BF16))
    u = uw[:, :, :LANES]
    wk = uw[:, :, LANES:]
    s_prev = state[...]
    sq = _bdot(jnp.concatenate([wk, q * e_cum], axis=1).astype(BF16), s_prev.astype(BF16))
    v_new = u - sq[:, :ch]
    o = sq[:, ch:] + _bdot(attn.astype(BF16), jnp.concatenate([v_new, zrows], axis=1).astype(BF16))
    cum_last = cum_b[:, ch - 1:ch, :]
    k_dec = (k * jnp.exp(cum_last - cum_b)).astype(BF16)
    v_new_b = v_new.astype(BF16)
    s_decayed = s_prev * jnp.exp(cum_last)
    on = o * lax.rsqrt(jnp.mean(o * o, axis=-1, keepdims=True) + EPS) * gn_ref[...][None]
    for i in range(hb):
        state[i] = s_decayed[i] + _dot_tn(k_dec[i], v_new_b[i])
        dz = dz_ref[:, i * dk:(i + 1) * dk]
        o_ref[:, i * dk:(i + 1) * dk] = (on[i] * (dz * _sigmoid(dz))).astype(o_ref.dtype)

    @pl.when(c == pl.num_programs(2) - 1)
    def _():
        sf_ref[...] = state[...]


def _delta(qkv, bb, cb, dz_arr, dz_col0, s0, g_dnorm, n_batch, seq):
    nh, dk, dv = s0.shape[1], s0.shape[2], s0.shape[3]
    ch = DELTA_CHUNK
    nc = seq // ch
    hb = max(c for c in (nh, 4, 2, 1) if nh % c == 0 and dz_col0 % (c * dk) == 0)
    wb = hb * dk
    ng = nh // hb
    row = lambda b, g, c: b * nc + c
    return pl.pallas_call(
        functools.partial(_delta_kernel, hb),
        grid=(n_batch, ng, nc),
        in_specs=[pl.BlockSpec((ch, wb), lambda b, g, c: (row(b, g, c), g)),
                  pl.BlockSpec((ch, wb), lambda b, g, c: (row(b, g, c), ng + g)),
                  pl.BlockSpec((ch, wb), lambda b, g, c: (row(b, g, c), 2 * ng + g)),
                  pl.BlockSpec((hb, ch, LANES), lambda b, g, c: (g, row(b, g, c), 0)),
                  pl.BlockSpec((hb, ch, LANES), lambda b, g, c: (g, row(b, g, c), 0)),
                  pl.BlockSpec((ch, wb), lambda b, g, c: (row(b, g, c), dz_col0 // wb + g)),
                  pl.BlockSpec((None, hb, dk, dv), lambda b, g, c: (b, g, 0, 0)),
                  pl.BlockSpec((1, dv), lambda b, g, c: (0, 0))],
        out_specs=[pl.BlockSpec((ch, wb), lambda b, g, c: (row(b, g, c), g)),
                   pl.BlockSpec((None, hb, dk, dv), lambda b, g, c: (b, g, 0, 0))],
        out_shape=[jax.ShapeDtypeStruct((n_batch * seq, nh * dv), BF16),
                   jax.ShapeDtypeStruct(s0.shape, F32)],
        scratch_shapes=[pltpu.VMEM((hb, dk, dv), F32)],
        compiler_params=_cparams(3),
    )(qkv, qkv, qkv, bb, cb, dz_arr, s0, g_dnorm.reshape(1, dv))


def _qkprep_kernel(nh, hd, q_ref, k_ref, v_ref, act_ref, gq_ref, gk_ref,
                   qn_ref, kn_ref, knb_ref, fc_ref, kh_ref, vh_ref, carry):
    t = pl.program_id(0)
    scale = hd ** -0.5
    for h in range(nh):
        sl = slice(h * hd, (h + 1) * hd)
        q = q_ref[:, sl]
        qn = q * lax.rsqrt(jnp.mean(q * q, axis=-1, keepdims=True) + EPS) * gq_ref[...]
        qn_ref[:, sl] = (qn * scale).astype(BF16)
        k = k_ref[:, sl]
        kn = k * lax.rsqrt(jnp.mean(k * k, axis=-1, keepdims=True) + EPS) * gk_ref[...]
        kn_ref[:, sl] = kn
        knb_ref[:, sl] = kn.astype(BF16)
        kh_ref[h] = kn
        vh_ref[h] = v_ref[:, sl]

    tm = act_ref.shape[0]
    tri = (_iota((tm, tm), 0) >= _iota((tm, tm), 1)).astype(BF16)
    fc = sum(_dot(tri, p) for p in _split3(act_ref[...])) + carry[0:1, :]
    fc_ref[...] = fc
    carry[...] = jnp.broadcast_to(fc[tm - 1:tm, :], carry.shape)


def _qkprep(z_f, act, g_q, g_k, nh, n_prompt, tiles_per_seq):
    r = z_f.shape[0]
    hd = g_q.shape[0]
    w = nh * hd
    n_tiles = r // TM
    n_ptiles = n_prompt * tiles_per_seq
    assert n_tiles == n_ptiles + 1
    rt = lambda t: jnp.where(t == 0, n_ptiles, t - 1)
    pt = lambda t: jnp.maximum(t - 1, 0)
    rows = lambda c: pl.BlockSpec((TM, w), lambda t: (rt(t), c))
    head_major = pl.BlockSpec((None, nh, TM, hd), lambda t: (pt(t) // tiles_per_seq, 0, pt(t) % tiles_per_seq, 0))

    def kern(*refs):
        carry = refs[-1]

        t = pl.program_id(0)

        @pl.when(jnp.logical_or(t == 0, (t - 1) % tiles_per_seq == 0))
        def _():
            carry[...] = jnp.zeros(carry.shape, F32)
        _qkprep_kernel(nh, hd, *refs)

    return pl.pallas_call(
        kern,
        grid=(n_tiles,),
        in_specs=[rows(0), rows(1), rows(2),
                  pl.BlockSpec((TM, LANES), lambda t: (rt(t), 0)),
                  pl.BlockSpec((1, hd), lambda t: (0, 0)),
                  pl.BlockSpec((1, hd), lambda t: (0, 0))],
        out_specs=[rows(0), rows(0), rows(0),
                   pl.BlockSpec((TM, LANES), lambda t: (rt(t), 0)),
                   head_major, head_major],
        out_shape=[jax.ShapeDtypeStruct((r, w), BF16),
                   jax.ShapeDtypeStruct((r, w), F32),
                   jax.ShapeDtypeStruct((r, w), BF16),
                   jax.ShapeDtypeStruct((r, LANES), F32),
                   jax.ShapeDtypeStruct((n_prompt, nh, tiles_per_seq * TM, hd), F32),
                   jax.ShapeDtypeStruct((n_prompt, nh, tiles_per_seq * TM, hd), F32)],
        scratch_shapes=[pltpu.VMEM((SUBLANES, LANES), F32)],
        compiler_params=_cparams(1),
    )(z_f, z_f, z_f, act, g_q.reshape(1, hd), g_k.reshape(1, hd))


def _flash_kernel(lane0, q_ref, k_ref, v_ref, fq_ref, fk_ref, o_ref, m_ref, l_ref, acc_ref, fqb_ref):
    h = pl.program_id(1)
    qi = pl.program_id(2)
    ki = pl.program_id(3)
    tq = q_ref.shape[0]
    tk = k_ref.shape[0]

    @pl.when(ki == 0)
    def _():
        m_ref[...] = jnp.full(m_ref.shape, NEG, F32)
        l_ref[...] = jnp.zeros(l_ref.shape, F32)
        acc_ref[...] = jnp.zeros(acc_ref.shape, F32)
        sel = (_iota((LANES, LANES), 0) == lane0 + h).astype(BF16)
        fqb_ref[...] = _select_dot(fq_ref[...], sel)

    @pl.when(ki <= qi)
    def _():
        s = _dot_nt(q_ref[...], k_ref[...])
        pick = (_iota((SUBLANES, LANES), 1) == lane0 + h).astype(BF16)
        fk_row = _select_dot(fk_ref[...], pick, nt=True)[0:1, :]
        fq = fqb_ref[...]
        s = s + jnp.concatenate([fq] * (tk // LANES), axis=1) - fk_row
        rows = qi * tq + _iota((tq, tk), 0)
        cols = ki * tk + _iota((tq, tk), 1)
        s = jnp.where(cols <= rows, s, NEG)
        m_prev = m_ref[...]
        m_new = jnp.maximum(m_prev, jnp.max(s, axis=-1, keepdims=True))
        alpha = jnp.exp(m_prev - m_new)
        p = jnp.exp(s - m_new)
        l_ref[...] = alpha * l_ref[...] + jnp.sum(p, axis=-1, keepdims=True)
        acc_ref[...] = alpha * acc_ref[...] + _dot(p.astype(BF16), v_ref[...].astype(BF16))
        m_ref[...] = m_new

    @pl.when(ki == pl.num_programs(3) - 1)
    def _():
        o_ref[...] = (acc_ref[...] / l_ref[...]).astype(o_ref.dtype)


def _flash(qn, knb, v_hm, fcum, nh, lane0, n_batch, seq, tq):
    hd = qn.shape[1] // nh
    nq = seq // tq
    qrow = lambda b, h, qi, ki: b * nq + qi
    krow = lambda b, h, qi, ki: b * nq + jnp.minimum(ki, qi)
    return pl.pallas_call(
        functools.partial(_flash_kernel, lane0),
        grid=(n_batch, nh, nq, nq),
        in_specs=[pl.BlockSpec((tq, hd), lambda b, h, qi, ki: (qrow(b, h, qi, ki), h)),
                  pl.BlockSpec((tq, hd), lambda b, h, qi, ki: (krow(b, h, qi, ki), h)),
                  pl.BlockSpec((None, None, tq, hd), lambda b, h, qi, ki: (b, h, jnp.minimum(ki, qi), 0)),
                  pl.BlockSpec((tq, LANES), lambda b, h, qi, ki: (qrow(b, h, qi, ki), 0)),
                  pl.BlockSpec((tq, LANES), lambda b, h, qi, ki: (krow(b, h, qi, ki), 0))],
        out_specs=pl.BlockSpec((tq, hd), lambda b, h, qi, ki: (qrow(b, h, qi, ki), h)),
        out_shape=jax.ShapeDtypeStruct((n_batch * seq, nh * hd), BF16),
        scratch_shapes=[pltpu.VMEM((tq, 1), F32), pltpu.VMEM((tq, 1), F32),
                        pltpu.VMEM((tq, hd), F32), pltpu.VMEM((tq, LANES), F32)],
        compiler_params=_cparams(4),
    )(qn, knb, v_hm, fcum, fcum)


def _decode_kernel(nh, pt_ref, q_ref, kn_ref, vn_ref, lfn_ref, lfa_ref, lfb_ref, ka_ref, kb_ref, va_ref, vb_ref,
                   o_ref, m_ref, l_ref, acc_ref, carry, qm, xpad):
    j = pl.program_id(1)
    page = carry.shape[1]
    hrow = _iota((HEAD_ROWS, LANES), 0)

    @pl.when(j == 0)
    def _():
        m_ref[...] = jnp.full(m_ref.shape, NEG, F32)
        l_ref[...] = jnp.zeros(l_ref.shape, F32)
        acc_ref[...] = jnp.zeros(acc_ref.shape, F32)
        carry[...] = jnp.zeros(carry.shape, F32)
        xpad[...] = jnp.zeros(xpad.shape, F32)
        q = q_ref[...]
        for h in range(nh):
            qm[h] = jnp.where(hrow == h, q, 0.0).astype(BF16)

    after = (_iota((page, page), 0) > _iota((page, page), 1)).astype(BF16)
    ones = jnp.ones((page, page), BF16)

    def page_update(lf_ref, k_ref, v_ref):
        xpad[0:nh, :] = lf_ref[...]
        parts = _split3(xpad[...])
        s = sum(_dot(p, after) for p in parts) + carry[...] + lfn_ref[...]
        carry[...] = carry[...] + sum(_dot(p, ones) for p in parts)
        for h in range(nh):
            s = s + _dot_nt(qm[h], k_ref[h].astype(BF16))
        m_prev = m_ref[...]
        m_new = jnp.maximum(m_prev, jnp.max(s, axis=-1, keepdims=True))
        alpha = jnp.exp(m_prev - m_new)
        p = jnp.exp(s - m_new)
        l_ref[...] = alpha * l_ref[...] + jnp.sum(p, axis=-1, keepdims=True)
        pv = jnp.zeros(acc_ref.shape, F32)
        for h in range(nh):
            pv = pv + _dot(jnp.where(hrow == h, p, 0.0).astype(BF16), v_ref[h].astype(BF16))
        acc_ref[...] = alpha * acc_ref[...] + pv
        m_ref[...] = m_new

    page_update(lfb_ref, kb_ref, vb_ref)
    page_update(lfa_ref, ka_ref, va_ref)

    @pl.when(j == pl.num_programs(1) - 1)
    def _():
        s_new = jnp.sum(q_ref[...] * kn_ref[...], axis=-1, keepdims=True)
        m_prev = m_ref[...]
        m_new = jnp.maximum(m_prev, s_new)
        alpha = jnp.exp(m_prev - m_new)
        pn = jnp.exp(s_new - m_new)
        l_fin = alpha * l_ref[...] + pn
        o_ref[...] = (alpha * acc_ref[...] + pn * vn_ref[...]) / l_fin


def _decode_attention(q, k_new, v_new, logf_new_b, logf_hm, k_hm, v_hm, layer, page_table):
    nb, _, hd = q.shape
    nh = k_hm.shape[2]
    npg = page_table.shape[1]
    page = k_hm.shape[3]
    assert npg % 2 == 0 and page == LANES and hd == LANES
    half = npg // 2
    small = pl.BlockSpec((None, HEAD_ROWS, hd), lambda b, j, pt: (b, 0, 0))
    pa = lambda b, j, pt: pt[b, npg - 2 - 2 * j]
    pb = lambda b, j, pt: pt[b, npg - 1 - 2 * j]
    lf = lambda pg: pl.BlockSpec((None, None, nh, page), lambda b, j, pt: (layer, pg(b, j, pt), 0, 0))
    cache = lambda pg: pl.BlockSpec((None, None, nh, page, hd), lambda b, j, pt: (layer, pg(b, j, pt), 0, 0, 0))
    grid_spec = pltpu.PrefetchScalarGridSpec(
        num_scalar_prefetch=1,
        grid=(nb, half),
        in_specs=[small, small, small,
                  pl.BlockSpec((None, HEAD_ROWS, page), lambda b, j, pt: (b, 0, 0)),
                  lf(pa), lf(pb), cache(pa), cache(pb), cache(pa), cache(pb)],
        out_specs=pl.BlockSpec((None, HEAD_ROWS, hd), lambda b, j, pt: (b, 0, 0)),
        scratch_shapes=[pltpu.VMEM((HEAD_ROWS, 1), F32), pltpu.VMEM((HEAD_ROWS, 1), F32),
                        pltpu.VMEM((HEAD_ROWS, hd), F32), pltpu.VMEM((HEAD_ROWS, page), F32),
                        pltpu.VMEM((nh, HEAD_ROWS, hd), BF16), pltpu.VMEM((HEAD_ROWS, page), F32)],
    )
    return pl.pallas_call(
        functools.partial(_decode_kernel, nh),
        grid_spec=grid_spec,
        out_shape=jax.ShapeDtypeStruct((nb, HEAD_ROWS, hd), F32),
        compiler_params=_cparams(2),
    )(page_table, q, k_new, v_new, logf_new_b, logf_hm, logf_hm, k_hm, k_hm, v_hm, v_hm)


def _merge_kernel(ya_ref, yb_ref, yc_ref, wa_ref, wb_ref, wc_ref, ga_ref, gb_ref, gc_ref, o_ref,
                  was, wbs, wcs):
    @pl.when(pl.program_id(1) == 0)
    def _():
        was[...] = wa_ref[...].astype(BF16)
        wbs[...] = wb_ref[...].astype(BF16)
        wcs[...] = wc_ref[...].astype(BF16)

    acc = ga_ref[...].astype(F32) * _dot(ya_ref[...], was[...])
    acc = acc + gb_ref[...].astype(F32) * _dot(yb_ref[...], wbs[...])
    acc = acc + gc_ref[...].astype(F32) * _dot(yc_ref[...], wcs[...])
    o_ref[...] = acc.astype(o_ref.dtype)


def _merge(ya, yb, yc, w_oa, w_ob, w_oc, layer, gates):
    r = ya.shape[0]
    ka, kb, kc = ya.shape[1], yb.shape[1], yc.shape[1]
    d = w_oa.shape[2]
    tn = _tile(d, 1024)
    nj = d // tn
    wspec = lambda k: pl.BlockSpec((None, k, tn), lambda j, i: (layer, 0, j))
    gspec = lambda o: pl.BlockSpec((TM, tn), lambda j, i: (i, o * nj + j))
    return pl.pallas_call(
        _merge_kernel,
        grid=(nj, r // TM),
        in_specs=[pl.BlockSpec((TM, ka), lambda j, i: (i, 0)),
                  pl.BlockSpec((TM, kb), lambda j, i: (i, 0)),
                  pl.BlockSpec((TM, kc), lambda j, i: (i, 0)),
                  wspec(ka), wspec(kb), wspec(kc), gspec(0), gspec(1), gspec(2)],
        out_specs=pl.BlockSpec((TM, tn), lambda j, i: (i, j)),
        out_shape=jax.ShapeDtypeStruct((r, d), BF16),
        scratch_shapes=[pltpu.VMEM((ka, tn), BF16), pltpu.VMEM((kb, tn), BF16), pltpu.VMEM((kc, tn), BF16)],
        compiler_params=_cparams(2),
    )(ya, yb, yc, w_oa, w_ob, w_oc, gates, gates, gates)


def _residual_mm_kernel(m_ref, w_ref, x_ref, gt_ref, o_ref, ws):
    @pl.when(pl.program_id(1) == 0)
    def _():
        ws[...] = w_ref[...].astype(BF16)

    y = _dot(m_ref[...], ws[...])
    tm, tn = y.shape
    y3 = y.reshape(tm // SUBLANES, SUBLANES, tn) * gt_ref[...][None]
    o_ref[...] = x_ref[...] + y3.reshape(tm, tn)


def _residual_mm(m, w_o, layer, x, gt, tiles_per_seq):
    r, k = m.shape
    d = w_o.shape[2]
    tn = _tile(d, 1024)
    n_prompt = gt.shape[0] - 1
    return pl.pallas_call(
        _residual_mm_kernel,
        grid=(d // tn, r // TM),
        in_specs=[pl.BlockSpec((TM, k), lambda j, i: (i, 0)),
                  pl.BlockSpec((None, k, tn), lambda j, i: (layer, 0, j)),
                  pl.BlockSpec((TM, tn), lambda j, i: (i, j)),
                  pl.BlockSpec((None, SUBLANES, tn),
                               lambda j, i: (jnp.minimum(i // tiles_per_seq, n_prompt), 0, j))],
        out_specs=pl.BlockSpec((TM, tn), lambda j, i: (i, j)),
        out_shape=jax.ShapeDtypeStruct((r, d), F32),
        scratch_shapes=[pltpu.VMEM((k, tn), BF16)],
        compiler_params=_cparams(2),
    )(m, w_o, x, gt)


def _gather_rows_kernel(tok_ref, na_ref, src_ref, o_ref, buf, sem):
    t = pl.program_id(0)
    n_act = na_ref[0]
    tm = o_ref.shape[0]

    def copy(tile, r, slot):
        return pltpu.make_async_copy(src_ref.at[pl.ds(tok_ref[tile * tm + r], 1)],
                                     buf.at[slot, pl.ds(r, 1)], sem.at[slot])

    def issue(tile, slot):
        def body(r, carry):
            copy(tile, r, slot).start()
            return carry
        lax.fori_loop(0, tm, body, 0)

    @pl.when(jnp.logical_and(t == 0, n_act > 0))
    def _():
        issue(0, 0)

    @pl.when(t + 1 < n_act)
    def _():
        issue(t + 1, (t + 1) % 2)

    @pl.when(t < n_act)
    def _():
        slot = t % 2

        def body(r, carry):
            copy(t, r, slot).wait()
            return carry
        lax.fori_loop(0, tm, body, 0)
        o_ref[...] = buf[slot]

    @pl.when(t >= n_act)
    def _():
        o_ref[...] = jnp.zeros(o_ref.shape, o_ref.dtype)


def _gather_rows(src, tok, n_active, n_out):
    w = src.shape[1]
    grid_spec = pltpu.PrefetchScalarGridSpec(
        num_scalar_prefetch=2,
        grid=(n_out // TM,),
        in_specs=[pl.BlockSpec(memory_space=pl.ANY)],
        out_specs=pl.BlockSpec((TM, w), lambda t, tok, na: (t, 0)),
        scratch_shapes=[pltpu.VMEM((2, TM, w), src.dtype), pltpu.SemaphoreType.DMA((2,))],
    )
    return pl.pallas_call(
        _gather_rows_kernel,
        grid_spec=grid_spec,
        out_shape=jax.ShapeDtypeStruct((n_out, w), src.dtype),
        compiler_params=_cparams(1, unchecked=True),
    )(tok, n_active, src)


def _unpack(xp):
    lo = pltpu.bitcast(xp << 16, F32).astype(BF16)
    hi = pltpu.bitcast(xp & jnp.uint32(0xFFFF0000), F32).astype(BF16)
    return lo, hi


def _expert_up_kernel(te_ref, na_ref, x_ref, wg_ref, wu_ref, bg_ref, bu_ref, o_ref, wgs, wus):
    t = pl.program_id(1)
    fresh = jnp.logical_or(t == 0, te_ref[t] != te_ref[jnp.maximum(t - 1, 0)])

    @pl.when(fresh)
    def _():
        wgs[...] = wg_ref[...].astype(BF16)
        wus[...] = wu_ref[...].astype(BF16)

    @pl.when(t < na_ref[0])
    def _():
        lo, hi = _unpack(x_ref[...])
        half = lo.shape[1]
        g = _dot(lo, wgs[0:half, :]) + _dot(hi, wgs[half:, :]) + bg_ref[...]
        u = _dot(lo, wus[0:half, :]) + _dot(hi, wus[half:, :]) + bu_ref[...]
        g = jnp.minimum(g, SWIGLU_LIMIT)
        u = jnp.clip(u, -SWIGLU_LIMIT, SWIGLU_LIMIT)
        o_ref[...] = (g * _sigmoid(SWIGLU_ALPHA * g) * (u + 1.0)).astype(o_ref.dtype)

    @pl.when(t >= na_ref[0])
    def _():
        o_ref[...] = jnp.zeros(o_ref.shape, o_ref.dtype)


def _expert_up(xs, w_gate, w_up, b_gate, b_up, layer, tile_expert, n_active):
    p, half = xs.shape
    n_exp, d, f = w_gate.shape[1], w_gate.shape[2], w_gate.shape[3]
    tf = _tile(f, 512)
    n_tiles = p // TM
    xrow = lambda fc, t, te, na: (jnp.minimum(t, na[0] - 1), 0)
    wspec = pl.BlockSpec((None, None, d, tf), lambda fc, t, te, na: (layer, te[t], 0, fc))
    bspec = pl.BlockSpec((None, None, 1, tf), lambda fc, t, te, na: (layer, te[t], 0, fc))
    grid_spec = pltpu.PrefetchScalarGridSpec(
        num_scalar_prefetch=2,
        grid=(f // tf, n_tiles),
        in_specs=[pl.BlockSpec((TM, half), xrow), wspec, wspec, bspec, bspec],
        out_specs=pl.BlockSpec((TM, tf), lambda fc, t, te, na: (t, fc)),
        scratch_shapes=[pltpu.VMEM((d, tf), BF16), pltpu.VMEM((d, tf), BF16)],
    )
    n_layers = w_gate.shape[0]
    return pl.pallas_call(
        _expert_up_kernel,
        grid_spec=grid_spec,
        out_shape=jax.ShapeDtypeStruct((p, f), BF16),
        compiler_params=_cparams(2),
    )(tile_expert, n_active, xs, w_gate, w_up,
      b_gate.reshape(n_layers, n_exp, 1, f), b_up.reshape(n_layers, n_exp, 1, f))


def _expert_down_kernel(te_ref, na_ref, h_ref, w_ref, b_ref, o_ref, ws):
    t = pl.program_id(0)
    fresh = jnp.logical_or(t == 0, te_ref[t] != te_ref[jnp.maximum(t - 1, 0)])

    @pl.when(fresh)
    def _():
        ws[...] = w_ref[...].astype(BF16)

    o_ref[...] = _dot(h_ref[...], ws[...]) + b_ref[...]


def _expert_down(hmid, w_down, b_down, layer, tile_expert, n_active):
    p, f = hmid.shape
    n_layers, n_exp, _, d = w_down.shape
    grid_spec = pltpu.PrefetchScalarGridSpec(
        num_scalar_prefetch=2,
        grid=(p // TM,),
        in_specs=[pl.BlockSpec((TM, f), lambda t, te, na: (t, 0)),
                  pl.BlockSpec((None, None, f, d), lambda t, te, na: (layer, te[t], 0, 0)),
                  pl.BlockSpec((None, None, 1, d), lambda t, te, na: (layer, te[t], 0, 0))],
        out_specs=pl.BlockSpec((TM, d), lambda t, te, na: (t, 0)),
        scratch_shapes=[pltpu.VMEM((f, d), BF16)],
    )
    return pl.pallas_call(
        _expert_down_kernel,
        grid_spec=grid_spec,
        out_shape=jax.ShapeDtypeStruct((p, d), F32),
        compiler_params=_cparams(1),
    )(tile_expert, n_active, hmid, w_down, b_down.reshape(n_layers, n_exp, 1, d))


COMBINE_ROWS = 64


def _combine_kernel(pos_ref, y_ref, x_ref, wt_ref, gt_ref, o_ref, buf, sem):
    i = pl.program_id(0)
    tm = x_ref.shape[0]

    def copy(r, k):
        src = pos_ref[(i * tm + r) * TOP_K + k]
        return pltpu.make_async_copy(y_ref.at[pl.ds(src, 1)], buf.at[k, pl.ds(r, 1)], sem)

    def issue(r, carry):
        for k in range(TOP_K):
            copy(r, k).start()
        return carry

    def drain(r, carry):
        for k in range(TOP_K):
            copy(r, k).wait()
        return carry

    lax.fori_loop(0, tm, issue, 0)
    lax.fori_loop(0, tm, drain, 0)
    wt = wt_ref[...]
    acc = wt[:, 0:1] * buf[0]
    for k in range(1, TOP_K):
        acc = acc + wt[:, k:k + 1] * buf[k]
    d = acc.shape[1]
    a3 = acc.reshape(tm // SUBLANES, SUBLANES, d) * gt_ref[...][None]
    o_ref[...] = x_ref[...] + a3.reshape(tm, d)


def _combine(y, pos, x, wts, gt, rows_per_seq):
    r, d = x.shape
    tm = COMBINE_ROWS
    n_prompt = gt.shape[0] - 1
    tiles_per_seq = rows_per_seq // tm
    grid_spec = pltpu.PrefetchScalarGridSpec(
        num_scalar_prefetch=1,
        grid=(r // tm,),
        in_specs=[pl.BlockSpec(memory_space=pl.ANY),
                  pl.BlockSpec((tm, d), lambda i, pos: (i, 0)),
                  pl.BlockSpec((tm, LANES), lambda i, pos: (i, 0)),
                  pl.BlockSpec((None, SUBLANES, d),
                               lambda i, pos: (jnp.minimum(i // tiles_per_seq, n_prompt), 0, 0))],
        out_specs=pl.BlockSpec((tm, d), lambda i, pos: (i, 0)),
        scratch_shapes=[pltpu.VMEM((TOP_K, tm, d), F32), pltpu.SemaphoreType.DMA(())],
    )
    return pl.pallas_call(
        _combine_kernel,
        grid_spec=grid_spec,
        out_shape=jax.ShapeDtypeStruct((r, d), F32),
        compiler_params=_cparams(1, unchecked=True),
    )(pos, y, x, wts, gt)


def _dispatch_plan(idx, n_tokens, n_experts, n_rows_max):
    eid = idx[:n_tokens, :TOP_K].reshape(-1)
    n_flat = eid.shape[0]
    order = jnp.argsort(eid, stable=True).astype(I32)
    sorted_e = eid[order]
    counts = jnp.sum((eid[:, None] == jnp.arange(n_experts, dtype=I32)[None, :]).astype(I32), axis=0)
    padded = ((counts + TM - 1) // TM) * TM
    ends_p = jnp.cumsum(padded)
    starts_p = ends_p - padded
    starts = jnp.cumsum(counts) - counts
    pos_sorted = starts_p[sorted_e] + (jnp.arange(n_flat, dtype=I32) - starts[sorted_e])
    row_token = jnp.zeros((n_rows_max,), I32).at[pos_sorted].set(order // TOP_K)
    pos_flat = jnp.zeros((n_flat,), I32).at[order].set(pos_sorted)
    n_active = (ends_p[-1] // TM).astype(I32)
    tile_start = jnp.arange(n_rows_max // TM, dtype=I32) * TM
    tile_expert = jnp.sum((tile_start[:, None] >= ends_p[None, :]).astype(I32), axis=1)
    tile_expert = jnp.minimum(tile_expert, n_experts - 1)
    last_e = jnp.max(jnp.where(counts > 0, jnp.arange(n_experts, dtype=I32), 0))
    tile_expert = jnp.where(tile_start < ends_p[-1], tile_expert, last_e).astype(I32)
    return row_token, pos_flat, tile_expert, n_active.reshape(1)


def _moe(x, h2p, idx, wts, gt, w_gate, b_gate, w_up, b_up, w_down, b_down, layer, n_tokens, rows_per_seq):
    r = x.shape[0]
    n_experts = w_gate.shape[1]
    n_rows_max = -(-(n_tokens * TOP_K + n_experts * (TM - 1)) // TM) * TM
    row_token, pos_flat, tile_expert, n_active = _dispatch_plan(idx, n_tokens, n_experts, n_rows_max)
    xs = _gather_rows(h2p, row_token, n_active, n_rows_max)
    hmid = _expert_up(xs, w_gate, w_up, b_gate, b_up, layer, tile_expert, n_active)
    y = _expert_down(hmid, w_down, b_down, layer, tile_expert, n_active)
    pos = jnp.zeros((r * TOP_K,), I32).at[:n_tokens * TOP_K].set(pos_flat)
    lane = jnp.arange(LANES)[None, :]
    wts = jnp.where((jnp.arange(r)[:, None] < n_tokens) & (lane < TOP_K), wts, 0.0)
    return _combine(y, pos, x, wts, gt, rows_per_seq)


def kernel(x_prompt, x_sample, c_prompt, c_sample, cache_k, cache_v, cache_logf, page_table, state_delta, state_delta_conv, state_sconv, w_ada, b_ada, g_mix, w_in, w_sconv, w_dconv, a_log, dt_bias, g_dnorm, g_qnorm, g_knorm, b_forget, w_oa, w_ob, w_oc, w_o, g_ffn, w_router, b_router, w_gate, b_gate, w_up, b_up, w_down, b_down):
    bp, seq, d = x_prompt.shape
    nb = x_sample.shape[0]
    assert nb == SUBLANES and x_sample.shape[1] == 1 and bp + nb <= 16
    n_layers = w_ada.shape[0]
    cw = w_sconv.shape[2]
    nh_d, dk, dv = state_delta.shape[2], state_delta.shape[3], state_delta.shape[4]
    nh_f, hd = cache_k.shape[3], cache_k.shape[4]
    page = cache_k.shape[2]
    assert dk == LANES and dv == LANES and hd == LANES and page == LANES
    assert 2 * nh_d + nh_f <= LANES and nh_f <= HEAD_ROWS
    rp = bp * seq
    r = rp + TM
    tiles_per_seq = seq // TM
    n_tokens = rp + nb
    qk_d, v_d = nh_d * dk, nh_d * dv
    f_d = nh_f * hd
    tq = 512 if seq % 512 == 0 else TM

    sizes = (cw, cw, cw, qk_d, qk_d, v_d, v_d, nh_d, nh_d, f_d, f_d, f_d, nh_f, d, d, d)
    off = np.concatenate([[0], np.cumsum(sizes)]).tolist()
    dz_col0 = off[6]

    x = jnp.concatenate([x_prompt.reshape(rp, d), x_sample.reshape(nb, d), jnp.zeros((TM - nb, d), F32)], axis=0)
    c_rows = jnp.concatenate([c_prompt, c_sample, jnp.zeros((16 - bp - nb, d), F32)], axis=0)
    mod_all = _ada(c_rows, w_ada, b_ada)
    logf_hm = jnp.transpose(cache_logf, (0, 1, 3, 2))
    k_hm = jnp.transpose(cache_k, (0, 1, 3, 2, 4))
    w_in_t = jnp.transpose(w_in, (2, 0, 1))
    v_hm = jnp.transpose(cache_v, (0, 1, 3, 2, 4))
    zero_state = jnp.zeros((bp, nh_d, dk, dv), F32)

    def with_sample(prompt_rows, sample_rows):
        pad = jnp.zeros((TM - nb, prompt_rows.shape[1]), prompt_rows.dtype)
        return jnp.concatenate([prompt_rows, sample_rows.astype(prompt_rows.dtype), pad], axis=0)

    def to_chunks(rows, n_rows=DELTA_CHUNK):
        return jnp.pad(rows[:, None, :], ((0, 0), (0, n_rows - 1), (0, 0))).reshape(nb * n_rows, -1)

    def head_rows(a):
        return jnp.pad(a, ((0, 0), (0, HEAD_ROWS - nh_f), (0, 0)))

    outs_p, outs_s = [], []
    for l in range(n_layers):
        mod = mod_all[l]
        modx = jnp.concatenate([jnp.broadcast_to(mod[:bp, None, :], (bp, SUBLANES, 6 * d)),
                                mod[None, bp:bp + nb, :]], axis=0)
        sh_m, sc_m, gt_m, sh_f, sc_f, gt_f = jnp.split(modx, 6, axis=-1)

        h = _norm(x, g_mix[l], sc_m, sh_m, tiles_per_seq)
        z_a = _in_proj(h, w_in_t, l, 0, off[7], False, F32)
        z_f = _in_proj(h, w_in_t, l, off[9], 3 * f_d, False, F32)
        gates = _in_proj(h, w_in_t, l, off[13], 3 * d, True, BF16)
        act = _in_small(h, w_in_t, l, off[7], off[12], a_log[l], dt_bias[l], b_forget[l])
        z_a_s = z_a[rp:rp + nb]
        act_s = act[rp:rp + nb]

        ya_p, sconv_p = _conv_prompt(z_a, w_sconv[l], 0, cw, bp, seq, True, BF16)
        ya_s, sconv_s = _conv_step(z_a_s, w_sconv[l], state_sconv[l], 0, cw, True, BF16)
        taps_a = w_sconv.shape[1]

        qkv_p, dconv_p = _conv_prompt(z_a, w_dconv[l], off[3], 2 * qk_d + v_d, bp, seq, False, F32)
        qkv_s, dconv_s = _conv_step(z_a_s, w_dconv[l], state_delta_conv[l], off[3], 2 * qk_d + v_d, False, F32)
        taps_b = w_dconv.shape[1]
        bb, cb = _delta_prep(act, nh_d)
        yb_p, delta_p = _delta(qkv_p, bb, cb, z_a, dz_col0, zero_state, g_dnorm[l], bp, seq)
        act_s_rows = to_chunks(jnp.where(jnp.arange(LANES)[None, :] < 2 * nh_d, act_s, 0.0))
        bb_s, cb_s = _delta_prep(act_s_rows, nh_d)
        yb_s_chunks, delta_s = _delta(to_chunks(qkv_s), bb_s, cb_s, to_chunks(z_a_s[:, dz_col0:dz_col0 + v_d]), 0,
                                      state_delta[l], g_dnorm[l], nb, DELTA_CHUNK)
        yb_s = yb_s_chunks[::DELTA_CHUNK]

        qn, kn, knb, fcum, k_hm_p, v_hm_p = _qkprep(z_f, act, g_qnorm[l], g_knorm[l], nh_f, bp, tiles_per_seq)
        yc_p = _flash(qn, knb, v_hm_p, fcum, nh_f, 2 * nh_d, bp, seq, tq)
        q_s = qn[rp:rp + nb].astype(F32).reshape(nb, nh_f, hd)
        k_s = kn[rp:rp + nb].reshape(nb, nh_f, hd)
        v_s = z_f[rp:rp + nb, 2 * f_d:].reshape(nb, nh_f, hd)
        logf_s = act_s[:, 2 * nh_d:2 * nh_d + nh_f]
        logf_s_b = jnp.broadcast_to(head_rows(logf_s[:, :, None]), (nb, HEAD_ROWS, page))
        yc_s = _decode_attention(head_rows(q_s), head_rows(k_s), head_rows(v_s), logf_s_b,
                                 logf_hm, k_hm, v_hm, l, page_table)[:, :nh_f].reshape(nb, f_d)

        ya = with_sample(ya_p, ya_s)
        yb = with_sample(yb_p, yb_s)
        yc = with_sample(yc_p, yc_s)
        merged = _merge(ya, yb, yc, w_oa, w_ob, w_oc, l, gates)
        x = _residual_mm(merged, w_o, l, x, gt_m, tiles_per_seq)

        h2p, idx, wts = _norm_router(x, g_ffn[l], sc_f, sh_f, w_router[l], b_router[l], tiles_per_seq)
        x = _moe(x, h2p, idx, wts, gt_f, w_gate, b_gate, w_up, b_up, w_down, b_down, l, n_tokens, seq)

        outs_p.append((jnp.transpose(k_hm_p, (0, 2, 1, 3)),
                       jnp.transpose(v_hm_p, (0, 2, 1, 3)),
                       act[:rp, 2 * nh_d:2 * nh_d + nh_f].reshape(bp, seq, nh_f),
                       delta_p,
                       dconv_p[:, SUBLANES - (taps_b - 1):],
                       sconv_p[:, SUBLANES - (taps_a - 1):]))
        outs_s.append((k_s.reshape(nb, 1, nh_f, hd), v_s.reshape(nb, 1, nh_f, hd), logf_s.reshape(nb, 1, nh_f),
                       delta_s, dconv_s, sconv_s))

    def stack(sts, i):
        return jnp.stack([st[i] for st in sts], axis=0)

    return (x[:rp].reshape(bp, seq, d), x[rp:rp + nb].reshape(nb, 1, d),
            stack(outs_p, 0), stack(outs_p, 1), stack(outs_p, 2), stack(outs_p, 3), stack(outs_p, 4), stack(outs_p, 5),
            stack(outs_s, 0), stack(outs_s, 1), stack(outs_s, 2), stack(outs_s, 3), stack(outs_s, 4), stack(outs_s, 5))
```

```python
import functools

import jax
import jax.numpy as jnp
import numpy as np
from jax import lax
from jax.experimental import pallas as pl
from jax.experimental.pallas import tpu as pltpu

F32 = jnp.float32
BF16 = jnp.bfloat16
I32 = jnp.int32
U32 = jnp.uint32
HI = lax.Precision.HIGHEST

EPS = 1e-6
TOP_K = 4
DELTA_CHUNK = 64
SWIGLU_ALPHA = 1.702
SWIGLU_LIMIT = 7.0
LANES = 128
SUBLANES = 8
HEAD_ROWS = 16
TM = 256
NEG = -1e30
VMEM_LIMIT = 56 * 1024 * 1024


def _cparams(n_axes, vmem=None, unchecked=False):
    return pltpu.CompilerParams(dimension_semantics=("arbitrary",) * n_axes,
                                vmem_limit_bytes=vmem or VMEM_LIMIT,
                                disable_bounds_checks=unchecked)


def _dot(a, b, precision=None):
    return jnp.dot(a, b, preferred_element_type=F32, precision=precision)


def _dot_nt(a, b, precision=None):
    return lax.dot_general(a, b, (((1,), (1,)), ((), ())), preferred_element_type=F32, precision=precision)


def _dot_tn(a, b, precision=None):
    return lax.dot_general(a, b, (((0,), (0,)), ((), ())), preferred_element_type=F32, precision=precision)


def _bdot(a, b):
    return lax.dot_general(a, b, (((2,), (1,)), ((0,), (0,))), preferred_element_type=F32)


def _bdot_nt(a, b):
    return lax.dot_general(a, b, (((2,), (2,)), ((0,), (0,))), preferred_element_type=F32)


def _split3(x):
    p1 = x.astype(BF16)
    r1 = x - p1.astype(F32)
    p2 = r1.astype(BF16)
    p3 = (r1 - p2.astype(F32)).astype(BF16)
    return p1, p2, p3


def _split2(x):
    hi = x.astype(BF16)
    return hi, (x - hi.astype(F32)).astype(BF16)


def _select_dot(x, sel, nt=False):
    d = _dot_nt if nt else _dot
    if nt:
        return sum(d(sel, p) for p in _split3(x))
    return sum(d(p, sel) for p in _split3(x))


def _dot3(a_hi, a_lo, b_hi, b_lo):
    return _dot(a_hi, b_hi) + (_dot(a_hi, b_lo) + _dot(a_lo, b_hi))


def _tile(n, preferred):
    t = preferred
    while n % t:
        t //= 2
    assert t >= LANES, (n, preferred)
    return t


def _iota(shape, dim):
    return lax.broadcasted_iota(I32, shape, dim)


def _sigmoid(x):
    return 1.0 / (1.0 + jnp.exp(-x))


def _ada_kernel(c_ref, w_ref, b_ref, o_ref):
    c = c_ref[...]
    a = (c * _sigmoid(c)).astype(BF16)
    o_ref[...] = _dot(a, w_ref[...].astype(BF16)) + b_ref[...]


def _ada(c_rows, w_ada, b_ada):
    n_layers, d, n = w_ada.shape
    tn = _tile(n, 512)
    return pl.pallas_call(
        _ada_kernel,
        grid=(n_layers, n // tn),
        in_specs=[pl.BlockSpec((16, d), lambda l, j: (0, 0)),
                  pl.BlockSpec((None, d, tn), lambda l, j: (l, 0, j)),
                  pl.BlockSpec((None, 1, tn), lambda l, j: (l, 0, j))],
        out_specs=pl.BlockSpec((None, 16, tn), lambda l, j: (l, 0, j)),
        out_shape=jax.ShapeDtypeStruct((n_layers, 16, n), F32),
        compiler_params=_cparams(2),
    )(c_rows, w_ada, b_ada.reshape(n_layers, 1, n))


def _modulated_norm(x, g, sc, sh):
    ms = jnp.mean(x * x, axis=-1, keepdims=True)
    y = x * lax.rsqrt(ms + EPS) * g
    tm, d = y.shape
    y3 = y.reshape(tm // SUBLANES, SUBLANES, d)
    return (y3 * (1.0 + sc[None]) + sh[None]).reshape(tm, d)


def _norm_kernel(x_ref, g_ref, sc_ref, sh_ref, o_ref):
    o_ref[...] = _modulated_norm(x_ref[...], g_ref[...], sc_ref[...], sh_ref[...]).astype(o_ref.dtype)


def _mod_index(tiles_per_seq, n_prompt):
    return lambda i: (jnp.minimum(i // tiles_per_seq, n_prompt), 0, 0)


def _norm(x, g, sc, sh, tiles_per_seq):
    r, d = x.shape
    n_prompt = sc.shape[0] - 1
    mod_spec = pl.BlockSpec((None, SUBLANES, d), _mod_index(tiles_per_seq, n_prompt))
    return pl.pallas_call(
        _norm_kernel,
        grid=(r // TM,),
        in_specs=[pl.BlockSpec((TM, d), lambda i: (i, 0)),
                  pl.BlockSpec((1, d), lambda i: (0, 0)),
                  mod_spec, mod_spec],
        out_specs=pl.BlockSpec((TM, d), lambda i: (i, 0)),
        out_shape=jax.ShapeDtypeStruct((r, d), BF16),
        compiler_params=_cparams(1),
    )(x, g.reshape(1, d), sc, sh)


def _norm_router_kernel(n_experts, x_ref, g_ref, sc_ref, sh_ref, wr_ref, br_ref, hp_ref, idx_ref, wt_ref):
    h = _modulated_norm(x_ref[...], g_ref[...], sc_ref[...], sh_ref[...])
    tm, d = h.shape
    lo = pltpu.bitcast(h[:, :d // 2].astype(BF16).astype(F32), U32)
    hi = pltpu.bitcast(h[:, d // 2:].astype(BF16).astype(F32), U32)
    hp_ref[...] = (hi & jnp.uint32(0xFFFF0000)) | (lo >> 16)
    logits = _dot(h, wr_ref[...], HI) + br_ref[...]
    lane = _iota(logits.shape, 1)
    work = jnp.where(lane < n_experts, logits, NEG)
    vals, idxs = [], []
    for _ in range(TOP_K):
        m = jnp.max(work, axis=-1, keepdims=True)
        ix = jnp.min(jnp.where(work == m, lane, LANES), axis=-1, keepdims=True)
        vals.append(m)
        idxs.append(ix)
        work = jnp.where(lane == ix, NEG, work)
    es = [jnp.exp(v - vals[0]) for v in vals]
    tot = es[0] + es[1] + es[2] + es[3]
    idx_out = jnp.zeros(logits.shape, I32)
    wt_out = jnp.zeros(logits.shape, F32)
    for k in range(TOP_K):
        idx_out = jnp.where(lane == k, idxs[k], idx_out)
        wt_out = jnp.where(lane == k, es[k] / tot, wt_out)
    idx_ref[...] = idx_out
    wt_ref[...] = wt_out


def _norm_router(x, g, sc, sh, w_router, b_router, tiles_per_seq):
    r, d = x.shape
    n_experts = w_router.shape[1]
    n_prompt = sc.shape[0] - 1
    wr = jnp.pad(w_router, ((0, 0), (0, LANES - n_experts)))
    br = jnp.pad(b_router, (0, LANES - n_experts)).reshape(1, LANES)
    mod_spec = pl.BlockSpec((None, SUBLANES, d), _mod_index(tiles_per_seq, n_prompt))
    return pl.pallas_call(
        functools.partial(_norm_router_kernel, n_experts),
        grid=(r // TM,),
        in_specs=[pl.BlockSpec((TM, d), lambda i: (i, 0)),
                  pl.BlockSpec((1, d), lambda i: (0, 0)),
                  mod_spec, mod_spec,
                  pl.BlockSpec((d, LANES), lambda i: (0, 0)),
                  pl.BlockSpec((1, LANES), lambda i: (0, 0))],
        out_specs=[pl.BlockSpec((TM, d // 2), lambda i: (i, 0)),
                   pl.BlockSpec((TM, LANES), lambda i: (i, 0)),
                   pl.BlockSpec((TM, LANES), lambda i: (i, 0))],
        out_shape=[jax.ShapeDtypeStruct((r, d // 2), U32),
                   jax.ShapeDtypeStruct((r, LANES), I32),
                   jax.ShapeDtypeStruct((r, LANES), F32)],
        compiler_params=_cparams(1),
    )(x, g.reshape(1, d), sc, sh, wr, br)


XPOSE = 512


def _transpose_to_bf16(src, dst):
    for kc in range(dst.shape[0] // XPOSE):
        dst[kc * XPOSE:(kc + 1) * XPOSE, :] = src[:, kc * XPOSE:(kc + 1) * XPOSE].T.astype(BF16)


def _in_proj_kernel(layer, row0, squash, h_ref, wt_ref, o_ref, wbuf, ws, sem):
    j = pl.program_id(0)
    i = pl.program_id(1)
    tn = ws.shape[1]

    def copy(tile, slot):
        return pltpu.make_async_copy(wt_ref.at[pl.ds(row0 + tile * tn, tn), layer], wbuf.at[slot], sem.at[slot])

    @pl.when(i == 0)
    def _():
        @pl.when(j == 0)
        def _():
            copy(0, 0).start()

        @pl.when(j + 1 < pl.num_programs(0))
        def _():
            copy(j + 1, (j + 1) % 2).start()

        slot = j % 2
        copy(j, slot).wait()
        _transpose_to_bf16(wbuf.at[slot], ws)

    y = _dot(h_ref[...], ws[...])
    if squash:
        y = _sigmoid(y)
    o_ref[...] = y.astype(o_ref.dtype)


def _in_proj(h, wt, layer, col0, width, squash, out_dtype):
    r, d = h.shape
    tn = _tile(width, 1024)
    assert d % XPOSE == 0 or d == XPOSE
    return pl.pallas_call(
        functools.partial(_in_proj_kernel, layer, col0, squash),
        grid=(width // tn, r // TM),
        in_specs=[pl.BlockSpec((TM, d), lambda j, i: (i, 0)),
                  pl.BlockSpec(memory_space=pl.ANY)],
        out_specs=pl.BlockSpec((TM, tn), lambda j, i: (i, j)),
        out_shape=jax.ShapeDtypeStruct((r, width), out_dtype),
        scratch_shapes=[pltpu.VMEM((2, tn, d), F32), pltpu.VMEM((d, tn), BF16), pltpu.SemaphoreType.DMA((2,))],
        compiler_params=_cparams(2),
    )(h, wt)


def _in_small_kernel(layer, row_a, row_b, nh_d, nh_f, h_ref, wt_ref, p_ref, o_ref, wbuf, was, wbs, sem):
    @pl.when(pl.program_id(0) == 0)
    def _():
        ca = pltpu.make_async_copy(wt_ref.at[pl.ds(row_a, LANES), layer], wbuf.at[0], sem.at[0])
        cb = pltpu.make_async_copy(wt_ref.at[pl.ds(row_b, LANES), layer], wbuf.at[1], sem.at[1])
        ca.start()
        cb.start()
        ca.wait()
        cb.wait()
        _transpose_to_bf16(wbuf.at[0], was)
        _transpose_to_bf16(wbuf.at[1], wbs)

    h = h_ref[...]
    lane = _iota((h.shape[0], LANES), 1)
    z = jnp.where(lane < 2 * nh_d, _dot(h, was[...]), _dot(h, wbs[...]))
    beta = _sigmoid(z)
    za = z + p_ref[1:2, :]
    softplus = jnp.maximum(za, 0.0) + jnp.log1p(jnp.exp(-jnp.abs(za)))
    g = -jnp.exp(p_ref[0:1, :]) * softplus
    zf = z + p_ref[2:3, :]
    logf = jnp.minimum(zf, 0.0) - jnp.log1p(jnp.exp(-jnp.abs(zf)))
    o_ref[...] = jnp.where(lane < nh_d, beta, jnp.where(lane < 2 * nh_d, g, logf))


def _in_small(h, wt, layer, col_ba, col_f, a_log, dt_bias, b_forget):
    r, d = h.shape
    nh_d, nh_f = a_log.shape[0], b_forget.shape[0]
    row_b = col_f - 2 * nh_d
    assert col_ba + LANES <= wt.shape[0] and row_b >= 0 and row_b + LANES <= wt.shape[0]
    p = jnp.zeros((SUBLANES, LANES), F32)
    p = p.at[0, nh_d:2 * nh_d].set(a_log).at[1, nh_d:2 * nh_d].set(dt_bias)
    p = p.at[2, 2 * nh_d:2 * nh_d + nh_f].set(b_forget)
    return pl.pallas_call(
        functools.partial(_in_small_kernel, layer, col_ba, row_b, nh_d, nh_f),
        grid=(r // TM,),
        in_specs=[pl.BlockSpec((TM, d), lambda i: (i, 0)),
                  pl.BlockSpec(memory_space=pl.ANY),
                  pl.BlockSpec((SUBLANES, LANES), lambda i: (0, 0))],
        out_specs=pl.BlockSpec((TM, LANES), lambda i: (i, 0)),
        out_shape=jax.ShapeDtypeStruct((r, LANES), F32),
        scratch_shapes=[pltpu.VMEM((2, LANES, d), F32), pltpu.VMEM((d, LANES), BF16),
                        pltpu.VMEM((d, LANES), BF16), pltpu.SemaphoreType.DMA((2,))],
        compiler_params=_cparams(1),
    )(h, wt, p)


def _conv_prompt_kernel(taps, gated, *refs):
    if gated:
        ab_ref, ac_ref, ah_ref, w_ref, y_ref, st_ref, ubuf = refs
    else:
        x_ref, w_ref, y_ref, st_ref, ubuf = refs
    s = pl.program_id(2)
    ts = y_ref.shape[0]

    @pl.when(s == 0)
    def _():
        ubuf[0:SUBLANES, :] = jnp.zeros((SUBLANES, ubuf.shape[1]), F32)

    u = ac_ref[...] * ah_ref[...] if gated else x_ref[...]
    ubuf[SUBLANES:SUBLANES + ts, :] = u
    base = SUBLANES - (taps - 1)
    acc = w_ref[0:1, :] * ubuf[base:base + ts, :]
    for i in range(1, taps):
        acc = acc + w_ref[i:i + 1, :] * ubuf[base + i:base + i + ts, :]
    y = ab_ref[...] * acc if gated else acc * _sigmoid(acc)
    y_ref[...] = y.astype(y_ref.dtype)
    tail = ubuf[ts:ts + SUBLANES, :]
    ubuf[0:SUBLANES, :] = tail

    @pl.when(s == pl.num_programs(2) - 1)
    def _():
        st_ref[...] = tail


def _conv_prompt(z, w, col0, width, n_batch, seq, gated, out_dtype):
    taps = w.shape[0]
    tc = _tile(np.gcd(width, col0) if col0 else width, 512)
    ts = TM
    nt = seq // ts
    grid = (width // tc, n_batch, nt)
    row = lambda j, b, s: b * nt + s
    if gated:
        ins = [pl.BlockSpec((ts, tc), lambda j, b, s, o=o: (row(j, b, s), (col0 + o * width) // tc + j))
               for o in range(3)]
        args = (z, z, z, w)
    else:
        ins = [pl.BlockSpec((ts, tc), lambda j, b, s: (row(j, b, s), col0 // tc + j))]
        args = (z, w)
    ins.append(pl.BlockSpec((taps, tc), lambda j, b, s: (0, j)))
    return pl.pallas_call(
        functools.partial(_conv_prompt_kernel, taps, gated),
        grid=grid,
        in_specs=ins,
        out_specs=[pl.BlockSpec((ts, tc), lambda j, b, s: (row(j, b, s), j)),
                   pl.BlockSpec((None, SUBLANES, tc), lambda j, b, s: (b, 0, j))],
        out_shape=[jax.ShapeDtypeStruct((n_batch * seq, width), out_dtype),
                   jax.ShapeDtypeStruct((n_batch, SUBLANES, width), F32)],
        scratch_shapes=[pltpu.VMEM((ts + SUBLANES, tc), F32)],
        compiler_params=_cparams(3),
    )(*args)


def _conv_step_kernel(taps, gated, *refs):
    if gated:
        ab_ref, ac_ref, ah_ref, w_ref, buf_ref, y_ref, nb_ref = refs
        u = ac_ref[...] * ah_ref[...]
    else:
        x_ref, w_ref, buf_ref, y_ref, nb_ref = refs
        u = x_ref[...]
    acc = w_ref[taps - 1:taps, :] * u
    for i in range(taps - 1):
        acc = acc + w_ref[i:i + 1, :] * buf_ref[i]
    y = ab_ref[...] * acc if gated else acc * _sigmoid(acc)
    y_ref[...] = y.astype(y_ref.dtype)
    for i in range(taps - 2):
        nb_ref[i] = buf_ref[i + 1]
    nb_ref[taps - 2] = u


def _conv_step(z_rows, w, buf, col0, width, gated, out_dtype):
    taps = w.shape[0]
    tc = _tile(np.gcd(width, col0) if col0 else width, 512)
    nb = z_rows.shape[0]
    buf_t = jnp.transpose(buf, (1, 0, 2))
    if gated:
        ins = [pl.BlockSpec((nb, tc), lambda j, o=o: (0, (col0 + o * width) // tc + j)) for o in range(3)]
        args = (z_rows, z_rows, z_rows, w, buf_t)
    else:
        ins = [pl.BlockSpec((nb, tc), lambda j: (0, col0 // tc + j))]
        args = (z_rows, w, buf_t)
    ins += [pl.BlockSpec((taps, tc), lambda j: (0, j)),
            pl.BlockSpec((taps - 1, nb, tc), lambda j: (0, 0, j))]
    y, nbuf = pl.pallas_call(
        functools.partial(_conv_step_kernel, taps, gated),
        grid=(width // tc,),
        in_specs=ins,
        out_specs=[pl.BlockSpec((nb, tc), lambda j: (0, j)),
                   pl.BlockSpec((taps - 1, nb, tc), lambda j: (0, 0, j))],
        out_shape=[jax.ShapeDtypeStruct((nb, width), out_dtype),
                   jax.ShapeDtypeStruct((taps - 1, nb, width), F32)],
        compiler_params=_cparams(1),
    )(*args)
    return y, jnp.transpose(nbuf, (1, 0, 2))


def _delta_prep_kernel(nh, ch, act_ref, bb_ref, cb_ref):
    act = act_ref[...]
    tm = act.shape[0]
    ii = _iota((tm, tm), 0)
    jj = _iota((tm, tm), 1)
    tri = jnp.logical_and(ii >= jj, ii // ch == jj // ch).astype(BF16)
    cum = sum(_dot(tri, p) for p in _split3(act))
    for h in range(nh):
        bb_ref[h] = jnp.broadcast_to(act[:, h:h + 1], (tm, LANES))
        cb_ref[h] = jnp.broadcast_to(cum[:, nh + h:nh + h + 1], (tm, LANES))


def _delta_prep(act, nh):
    r = act.shape[0]
    return pl.pallas_call(
        functools.partial(_delta_prep_kernel, nh, DELTA_CHUNK),
        grid=(r // TM,),
        in_specs=[pl.BlockSpec((TM, LANES), lambda i: (i, 0))],
        out_specs=[pl.BlockSpec((nh, TM, LANES), lambda i: (0, i, 0)),
                   pl.BlockSpec((nh, TM, LANES), lambda i: (0, i, 0))],
        out_shape=[jax.ShapeDtypeStruct((nh, r, LANES), F32),
                   jax.ShapeDtypeStruct((nh, r, LANES), F32)],
        compiler_params=_cparams(1),
    )(act)


def _delta_kernel(hb, q_ref, k_ref, v_ref, bb_ref, cb_ref, dz_ref, s0_ref, gn_ref, o_ref, sf_ref, state):
    c = pl.program_id(2)
    ch = q_ref.shape[0]
    dk = LANES

    @pl.when(c == 0)
    def _():
        state[...] = s0_ref[...]

    def heads(ref):
        return jnp.stack([ref[:, i * dk:(i + 1) * dk] for i in range(hb)], axis=0)

    ii = _iota((ch, LANES), 0)[None]
    jj = _iota((ch, LANES), 1)[None]
    left = jj < ch
    pick0 = jnp.broadcast_to((jj == 0).astype(BF16), (hb, ch, LANES))
    eye_right = (jj - ch == ii).astype(F32)
    zrows = jnp.zeros((hb, ch, LANES), F32)
    zrows_b = jnp.zeros((hb, ch, LANES), BF16)
    n_levels = int(np.log2(ch))

    q = heads(q_ref)
    k = heads(k_ref)
    q = q * lax.rsqrt(jnp.sum(q * q, axis=-1, keepdims=True) + EPS) * (dk ** -0.5)
    k = k * lax.rsqrt(jnp.sum(k * k, axis=-1, keepdims=True) + EPS)
    beta_b = bb_ref[...]
    cum_b = cb_ref[...]
    cum_r = sum(_bdot_nt(pick0, p) for p in _split3(jnp.concatenate([cum_b, zrows], axis=1)))
    diff = cum_b - cum_r
    decay_strict = jnp.exp(jnp.where(jnp.logical_and(left, ii > jj), diff, NEG))
    decay_incl = jnp.exp(jnp.where(jnp.logical_and(left, ii >= jj), diff, NEG))
    kb = k * beta_b
    k_pad = jnp.concatenate([k, zrows], axis=1).astype(BF16)
    la = _bdot_nt(jnp.concatenate([kb, q], axis=1).astype(BF16), k_pad)
    lmat = la[:, :ch] * decay_strict
    attn = la[:, ch:] * decay_incl
    w = jnp.where(left, -lmat, eye_right)
    for _ in range(n_levels):
        w_hi, w_lo = _split2(w)
        p_hi = jnp.where(left, w_hi, jnp.zeros_like(w_hi))
        p_lo = jnp.where(left, w_lo, jnp.zeros_like(w_lo))
        r_hi = jnp.concatenate([w_hi, zrows_b], axis=1)
        r_lo = jnp.concatenate([w_lo, zrows_b], axis=1)
        r = _bdot(p_hi, r_hi) + (_bdot(p_hi, r_lo) + _bdot(p_lo, r_hi))
        w = jnp.where(left, r, w + r)
    t_right = jnp.where(left, 0.0, w).astype(BF16)
    e_cum = jnp.exp(cum_b)
    rhs = jnp.concatenate([heads(v_ref) * beta_b, kb * e_cum], axis=2)
    uw = _bdot(t_right, jnp.concatenate([jnp.zeros_like(rhs), rhs], axis=1).astype(BF16))
    u = uw[:, :, :LANES]
    wk = uw[:, :, LANES:]
    s_prev = state[...]
    sq = _bdot(jnp.concatenate([wk, q * e_cum], axis=1).astype(BF16), s_prev.astype(BF16))
    v_new = u - sq[:, :ch]
    o = sq[:, ch:] + _bdot(attn.astype(BF16), jnp.concatenate([v_new, zrows], axis=1).astype(BF16))
    cum_last = cum_b[:, ch - 1:ch, :]
    k_dec = (k * jnp.exp(cum_last - cum_b)).astype(BF16)
    v_new_b = v_new.astype(BF16)
    s_decayed = s_prev * jnp.exp(cum_last)
    on = o * lax.rsqrt(jnp.mean(o * o, axis=-1, keepdims=True) + EPS) * gn_ref[...][None]
    for i in range(hb):
        state[i] = s_decayed[i] + _dot_tn(k_dec[i], v_new_b[i])
        dz = dz_ref[:, i * dk:(i + 1) * dk]
        o_ref[:, i * dk:(i + 1) * dk] = (on[i] * (dz * _sigmoid(dz))).astype(o_ref.dtype)

    @pl.when(c == pl.num_programs(2) - 1)
    def _():
        sf_ref[...] = state[...]


def _delta(qkv, bb, cb, dz_arr, dz_col0, s0, g_dnorm, n_batch, seq):
    nh, dk, dv = s0.shape[1], s0.shape[2], s0.shape[3]
    ch = DELTA_CHUNK
    nc = seq // ch
    hb = max(c for c in (nh, 4, 2, 1) if nh % c == 0 and dz_col0 % (c * dk) == 0)
    wb = hb * dk
    ng = nh // hb
    row = lambda b, g, c: b * nc + c
    return pl.pallas_call(
        functools.partial(_delta_kernel, hb),
        grid=(n_batch, ng, nc),
        in_specs=[pl.BlockSpec((ch, wb), lambda b, g, c: (row(b, g, c), g)),
                  pl.BlockSpec((ch, wb), lambda b, g, c: (row(b, g, c), ng + g)),
                  pl.BlockSpec((ch, wb), lambda b, g, c: (row(b, g, c), 2 * ng + g)),
                  pl.BlockSpec((hb, ch, LANES), lambda b, g, c: (g, row(b, g, c), 0)),
                  pl.BlockSpec((hb, ch, LANES), lambda b, g, c: (g, row(b, g, c), 0)),
                  pl.BlockSpec((ch, wb), lambda b, g, c: (row(b, g, c), dz_col0 // wb + g)),
                  pl.BlockSpec((None, hb, dk, dv), lambda b, g, c: (b, g, 0, 0)),
                  pl.BlockSpec((1, dv), lambda b, g, c: (0, 0))],
        out_specs=[pl.BlockSpec((ch, wb), lambda b, g, c: (row(b, g, c), g)),
                   pl.BlockSpec((None, hb, dk, dv), lambda b, g, c: (b, g, 0, 0))],
        out_shape=[jax.ShapeDtypeStruct((n_batch * seq, nh * dv), BF16),
                   jax.ShapeDtypeStruct(s0.shape, F32)],
        scratch_shapes=[pltpu.VMEM((hb, dk, dv), F32)],
        compiler_params=_cparams(3),
    )(qkv, qkv, qkv, bb, cb, dz_arr, s0, g_dnorm.reshape(1, dv))


def _qkprep_kernel(nh, hd, q_ref, k_ref, v_ref, act_ref, gq_ref, gk_ref,
                   qn_ref, kn_ref, knb_ref, fc_ref, kh_ref, vh_ref, carry):
    t = pl.program_id(0)
    scale = hd ** -0.5
    for h in range(nh):
        sl = slice(h * hd, (h + 1) * hd)
        q = q_ref[:, sl]
        qn = q * lax.rsqrt(jnp.mean(q * q, axis=-1, keepdims=True) + EPS) * gq_ref[...]
        qn_ref[:, sl] = (qn * scale).astype(BF16)
        k = k_ref[:, sl]
        kn = k * lax.rsqrt(jnp.mean(k * k, axis=-1, keepdims=True) + EPS) * gk_ref[...]
        kn_ref[:, sl] = kn
        knb_ref[:, sl] = kn.astype(BF16)
        kh_ref[h] = kn
        vh_ref[h] = v_ref[:, sl]

    tm = act_ref.shape[0]
    tri = (_iota((tm, tm), 0) >= _iota((tm, tm), 1)).astype(BF16)
    fc = sum(_dot(tri, p) for p in _split3(act_ref[...])) + carry[0:1, :]
    fc_ref[...] = fc
    carry[...] = jnp.broadcast_to(fc[tm - 1:tm, :], carry.shape)


def _qkprep(z_f, act, g_q, g_k, nh, n_prompt, tiles_per_seq):
    r = z_f.shape[0]
    hd = g_q.shape[0]
    w = nh * hd
    n_tiles = r // TM
    n_ptiles = n_prompt * tiles_per_seq
    assert n_tiles == n_ptiles + 1
    rt = lambda t: jnp.where(t == 0, n_ptiles, t - 1)
    pt = lambda t: jnp.maximum(t - 1, 0)
    rows = lambda c: pl.BlockSpec((TM, w), lambda t: (rt(t), c))
    head_major = pl.BlockSpec((None, nh, TM, hd), lambda t: (pt(t) // tiles_per_seq, 0, pt(t) % tiles_per_seq, 0))

    def kern(*refs):
        carry = refs[-1]

        t = pl.program_id(0)

        @pl.when(jnp.logical_or(t == 0, (t - 1) % tiles_per_seq == 0))
        def _():
            carry[...] = jnp.zeros(carry.shape, F32)
        _qkprep_kernel(nh, hd, *refs)

    return pl.pallas_call(
        kern,
        grid=(n_tiles,),
        in_specs=[rows(0), rows(1), rows(2),
                  pl.BlockSpec((TM, LANES), lambda t: (rt(t), 0)),
                  pl.BlockSpec((1, hd), lambda t: (0, 0)),
                  pl.BlockSpec((1, hd), lambda t: (0, 0))],
        out_specs=[rows(0), rows(0), rows(0),
                   pl.BlockSpec((TM, LANES), lambda t: (rt(t), 0)),
                   head_major, head_major],
        out_shape=[jax.ShapeDtypeStruct((r, w), BF16),
                   jax.ShapeDtypeStruct((r, w), F32),
                   jax.ShapeDtypeStruct((r, w), BF16),
                   jax.ShapeDtypeStruct((r, LANES), F32),
                   jax.ShapeDtypeStruct((n_prompt, nh, tiles_per_seq * TM, hd), F32),
                   jax.ShapeDtypeStruct((n_prompt, nh, tiles_per_seq * TM, hd), F32)],
        scratch_shapes=[pltpu.VMEM((SUBLANES, LANES), F32)],
        compiler_params=_cparams(1),
    )(z_f, z_f, z_f, act, g_q.reshape(1, hd), g_k.reshape(1, hd))


def _flash_kernel(lane0, qi_ref, ki_ref, q_ref, k_ref, v_ref, fq_ref, fk_ref, o_ref, m_ref, l_ref, acc_ref, fqb_ref):
    h = pl.program_id(1)
    qi = qi_ref[pl.program_id(2)]
    ki = ki_ref[pl.program_id(2)]
    tq = q_ref.shape[0]
    tk = k_ref.shape[0]

    @pl.when(ki == 0)
    def _():
        m_ref[...] = jnp.full(m_ref.shape, NEG, F32)
        l_ref[...] = jnp.zeros(l_ref.shape, F32)
        acc_ref[...] = jnp.zeros(acc_ref.shape, F32)
        sel = (_iota((LANES, LANES), 0) == lane0 + h).astype(BF16)
        fqb_ref[...] = _select_dot(fq_ref[...], sel)

    def update(on_diagonal):
        s = _dot_nt(q_ref[...], k_ref[...])
        pick = (_iota((SUBLANES, LANES), 1) == lane0 + h).astype(BF16)
        fk_row = _select_dot(fk_ref[...], pick, nt=True)[0:1, :]
        fq = fqb_ref[...]
        s = s + jnp.concatenate([fq] * (tk // LANES), axis=1) - fk_row
        if on_diagonal:
            s = jnp.where(_iota((tq, tk), 1) <= _iota((tq, tk), 0), s, NEG)
        m_prev = m_ref[...]
        m_new = jnp.maximum(m_prev, jnp.max(s, axis=-1, keepdims=True))
        alpha = jnp.exp(m_prev - m_new)
        p = jnp.exp(s - m_new)
        l_ref[...] = alpha * l_ref[...] + jnp.sum(p, axis=-1, keepdims=True)
        acc_ref[...] = alpha * acc_ref[...] + _dot(p.astype(BF16), v_ref[...].astype(BF16))
        m_ref[...] = m_new

    @pl.when(ki < qi)
    def _():
        update(False)

    @pl.when(ki == qi)
    def _():
        update(True)
        o_ref[...] = (acc_ref[...] / l_ref[...]).astype(o_ref.dtype)


def _flash(qn, knb, v_hm, fcum, nh, lane0, n_batch, seq, tq):
    hd = qn.shape[1] // nh
    nq = seq // tq
    pairs = [(qi, ki) for qi in range(nq) for ki in range(qi + 1)]
    qi_arr = jnp.asarray([p[0] for p in pairs], I32)
    ki_arr = jnp.asarray([p[1] for p in pairs], I32)
    qrow = lambda b, h, p, qa, ka: b * nq + qa[p]
    krow = lambda b, h, p, qa, ka: b * nq + ka[p]
    grid_spec = pltpu.PrefetchScalarGridSpec(
        num_scalar_prefetch=2,
        grid=(n_batch, nh, len(pairs)),
        in_specs=[pl.BlockSpec((tq, hd), lambda b, h, p, qa, ka: (qrow(b, h, p, qa, ka), h)),
                  pl.BlockSpec((tq, hd), lambda b, h, p, qa, ka: (krow(b, h, p, qa, ka), h)),
                  pl.BlockSpec((None, None, tq, hd), lambda b, h, p, qa, ka: (b, h, ka[p], 0)),
                  pl.BlockSpec((tq, LANES), lambda b, h, p, qa, ka: (qrow(b, h, p, qa, ka), 0)),
                  pl.BlockSpec((tq, LANES), lambda b, h, p, qa, ka: (krow(b, h, p, qa, ka), 0))],
        out_specs=pl.BlockSpec((tq, hd), lambda b, h, p, qa, ka: (qrow(b, h, p, qa, ka), h)),
        scratch_shapes=[pltpu.VMEM((tq, 1), F32), pltpu.VMEM((tq, 1), F32),
                        pltpu.VMEM((tq, hd), F32), pltpu.VMEM((tq, LANES), F32)],
    )
    return pl.pallas_call(
        functools.partial(_flash_kernel, lane0),
        grid_spec=grid_spec,
        out_shape=jax.ShapeDtypeStruct((n_batch * seq, nh * hd), BF16),
        compiler_params=_cparams(3),
    )(qi_arr, ki_arr, qn, knb, v_hm, fcum, fcum)


def _decode_kernel(nh, pt_ref, q_ref, kn_ref, vn_ref, lfn_ref, lfa_ref, lfb_ref, ka_ref, kb_ref, va_ref, vb_ref,
                   o_ref, m_ref, l_ref, acc_ref, carry, xpad):
    j = pl.program_id(1)
    page = carry.shape[1]
    hrow = _iota((HEAD_ROWS, LANES), 0)

    @pl.when(j == 0)
    def _():
        m_ref[...] = jnp.full(m_ref.shape, NEG, F32)
        l_ref[...] = jnp.zeros(l_ref.shape, F32)
        acc_ref[...] = jnp.zeros(acc_ref.shape, F32)
        carry[...] = jnp.zeros(carry.shape, F32)
        xpad[...] = jnp.zeros(xpad.shape, F32)

    after = (_iota((page, page), 0) > _iota((page, page), 1)).astype(BF16)
    ones = jnp.ones((page, page), BF16)
    qb = q_ref[...].astype(BF16)

    def page_update(lf_ref, k_ref, v_ref):
        xpad[0:nh, :] = lf_ref[...]
        parts = _split3(xpad[...])
        s = sum(_dot(p, after) for p in parts) + carry[...] + lfn_ref[...]
        carry[...] = carry[...] + sum(_dot(p, ones) for p in parts)
        s_all = _dot_nt(qb, k_ref[...].reshape(nh * page, qb.shape[1]).astype(BF16))
        for h in range(nh):
            s = s + jnp.where(hrow == h, s_all[:, h * page:(h + 1) * page], 0.0)
        m_prev = m_ref[...]
        m_new = jnp.maximum(m_prev, jnp.max(s, axis=-1, keepdims=True))
        alpha = jnp.exp(m_prev - m_new)
        p = jnp.exp(s - m_new)
        l_ref[...] = alpha * l_ref[...] + jnp.sum(p, axis=-1, keepdims=True)
        p_blocks = jnp.concatenate([jnp.where(hrow == h, p, 0.0) for h in range(nh)], axis=1).astype(BF16)
        pv = _dot(p_blocks, v_ref[...].reshape(nh * page, qb.shape[1]).astype(BF16))
        acc_ref[...] = alpha * acc_ref[...] + pv
        m_ref[...] = m_new

    page_update(lfb_ref, kb_ref, vb_ref)
    page_update(lfa_ref, ka_ref, va_ref)

    @pl.when(j == pl.num_programs(1) - 1)
    def _():
        s_new = jnp.sum(q_ref[...] * kn_ref[...], axis=-1, keepdims=True)
        m_prev = m_ref[...]
        m_new = jnp.maximum(m_prev, s_new)
        alpha = jnp.exp(m_prev - m_new)
        pn = jnp.exp(s_new - m_new)
        l_fin = alpha * l_ref[...] + pn
        o_ref[...] = (alpha * acc_ref[...] + pn * vn_ref[...]) / l_fin


def _decode_attention(q, k_new, v_new, logf_new_b, logf_hm, k_hm, v_hm, layer, page_table):
    nb, _, hd = q.shape
    nh = k_hm.shape[2]
    npg = page_table.shape[1]
    page = k_hm.shape[3]
    assert npg % 2 == 0 and page == LANES and hd == LANES
    half = npg // 2
    small = pl.BlockSpec((None, HEAD_ROWS, hd), lambda b, j, pt: (b, 0, 0))
    pa = lambda b, j, pt: pt[b, npg - 2 - 2 * j]
    pb = lambda b, j, pt: pt[b, npg - 1 - 2 * j]
    lf = lambda pg: pl.BlockSpec((None, None, nh, page), lambda b, j, pt: (layer, pg(b, j, pt), 0, 0))
    cache = lambda pg: pl.BlockSpec((None, None, nh, page, hd), lambda b, j, pt: (layer, pg(b, j, pt), 0, 0, 0))
    grid_spec = pltpu.PrefetchScalarGridSpec(
        num_scalar_prefetch=1,
        grid=(nb, half),
        in_specs=[small, small, small,
                  pl.BlockSpec((None, HEAD_ROWS, page), lambda b, j, pt: (b, 0, 0)),
                  lf(pa), lf(pb), cache(pa), cache(pb), cache(pa), cache(pb)],
        out_specs=pl.BlockSpec((None, HEAD_ROWS, hd), lambda b, j, pt: (b, 0, 0)),
        scratch_shapes=[pltpu.VMEM((HEAD_ROWS, 1), F32), pltpu.VMEM((HEAD_ROWS, 1), F32),
                        pltpu.VMEM((HEAD_ROWS, hd), F32), pltpu.VMEM((HEAD_ROWS, page), F32),
                        pltpu.VMEM((HEAD_ROWS, page), F32)],
    )
    return pl.pallas_call(
        functools.partial(_decode_kernel, nh),
        grid_spec=grid_spec,
        out_shape=jax.ShapeDtypeStruct((nb, HEAD_ROWS, hd), F32),
        compiler_params=_cparams(2),
    )(page_table, q, k_new, v_new, logf_new_b, logf_hm, logf_hm, k_hm, k_hm, v_hm, v_hm)


def _merge_kernel(ya_ref, yb_ref, yc_ref, wa_ref, wb_ref, wc_ref, ga_ref, gb_ref, gc_ref, o_ref,
                  was, wbs, wcs):
    @pl.when(pl.program_id(1) == 0)
    def _():
        was[...] = wa_ref[...].astype(BF16)
        wbs[...] = wb_ref[...].astype(BF16)
        wcs[...] = wc_ref[...].astype(BF16)

    acc = ga_ref[...].astype(F32) * _dot(ya_ref[...], was[...])
    acc = acc + gb_ref[...].astype(F32) * _dot(yb_ref[...], wbs[...])
    acc = acc + gc_ref[...].astype(F32) * _dot(yc_ref[...], wcs[...])
    o_ref[...] = acc.astype(o_ref.dtype)


def _merge(ya, yb, yc, w_oa, w_ob, w_oc, layer, gates):
    r = ya.shape[0]
    ka, kb, kc = ya.shape[1], yb.shape[1], yc.shape[1]
    d = w_oa.shape[2]
    tn = _tile(d, 1024)
    nj = d // tn
    wspec = lambda k: pl.BlockSpec((None, k, tn), lambda j, i: (layer, 0, j))
    gspec = lambda o: pl.BlockSpec((TM, tn), lambda j, i: (i, o * nj + j))
    return pl.pallas_call(
        _merge_kernel,
        grid=(nj, r // TM),
        in_specs=[pl.BlockSpec((TM, ka), lambda j, i: (i, 0)),
                  pl.BlockSpec((TM, kb), lambda j, i: (i, 0)),
                  pl.BlockSpec((TM, kc), lambda j, i: (i, 0)),
                  wspec(ka), wspec(kb), wspec(kc), gspec(0), gspec(1), gspec(2)],
        out_specs=pl.BlockSpec((TM, tn), lambda j, i: (i, j)),
        out_shape=jax.ShapeDtypeStruct((r, d), BF16),
        scratch_shapes=[pltpu.VMEM((ka, tn), BF16), pltpu.VMEM((kb, tn), BF16), pltpu.VMEM((kc, tn), BF16)],
        compiler_params=_cparams(2),
    )(ya, yb, yc, w_oa, w_ob, w_oc, gates, gates, gates)


def _residual_mm_kernel(m_ref, w_ref, x_ref, gt_ref, o_ref, ws):
    @pl.when(pl.program_id(1) == 0)
    def _():
        ws[...] = w_ref[...].astype(BF16)

    y = _dot(m_ref[...], ws[...])
    tm, tn = y.shape
    y3 = y.reshape(tm // SUBLANES, SUBLANES, tn) * gt_ref[...][None]
    o_ref[...] = x_ref[...] + y3.reshape(tm, tn)


def _residual_mm(m, w_o, layer, x, gt, tiles_per_seq):
    r, k = m.shape
    d = w_o.shape[2]
    tn = _tile(d, 1024)
    n_prompt = gt.shape[0] - 1
    return pl.pallas_call(
        _residual_mm_kernel,
        grid=(d // tn, r // TM),
        in_specs=[pl.BlockSpec((TM, k), lambda j, i: (i, 0)),
                  pl.BlockSpec((None, k, tn), lambda j, i: (layer, 0, j)),
                  pl.BlockSpec((TM, tn), lambda j, i: (i, j)),
                  pl.BlockSpec((None, SUBLANES, tn),
                               lambda j, i: (jnp.minimum(i // tiles_per_seq, n_prompt), 0, j))],
        out_specs=pl.BlockSpec((TM, tn), lambda j, i: (i, j)),
        out_shape=jax.ShapeDtypeStruct((r, d), F32),
        scratch_shapes=[pltpu.VMEM((k, tn), BF16)],
        compiler_params=_cparams(2),
    )(m, w_o, x, gt)


ISSUE_UNROLL = 8


def _gather_rows_kernel(tok_ref, na_ref, src_ref, o_ref, buf, sem):
    t = pl.program_id(0)
    n_act = na_ref[0]
    tm = o_ref.shape[0]

    def copy(tile, r, slot):
        return pltpu.make_async_copy(src_ref.at[pl.ds(tok_ref[tile * tm + r], 1)],
                                     buf.at[slot, pl.ds(r, 1)], sem.at[slot])

    def issue(tile, slot):
        def body(r, carry):
            copy(tile, r, slot).start()
            return carry
        lax.fori_loop(0, tm, body, 0, unroll=ISSUE_UNROLL)

    @pl.when(jnp.logical_and(t == 0, n_act > 0))
    def _():
        issue(0, 0)

    @pl.when(t + 1 < n_act)
    def _():
        issue(t + 1, (t + 1) % 2)

    @pl.when(t < n_act)
    def _():
        slot = t % 2
        pltpu.make_async_copy(buf.at[slot], buf.at[slot], sem.at[slot]).wait()
        o_ref[...] = buf[slot]

    @pl.when(t >= n_act)
    def _():
        o_ref[...] = jnp.zeros(o_ref.shape, o_ref.dtype)


def _gather_rows(src, tok, n_active, n_out):
    w = src.shape[1]
    grid_spec = pltpu.PrefetchScalarGridSpec(
        num_scalar_prefetch=2,
        grid=(n_out // TM,),
        in_specs=[pl.BlockSpec(memory_space=pl.ANY)],
        out_specs=pl.BlockSpec((TM, w), lambda t, tok, na: (t, 0)),
        scratch_shapes=[pltpu.VMEM((2, TM, w), src.dtype), pltpu.SemaphoreType.DMA((2,))],
    )
    return pl.pallas_call(
        _gather_rows_kernel,
        grid_spec=grid_spec,
        out_shape=jax.ShapeDtypeStruct((n_out, w), src.dtype),
        compiler_params=_cparams(1, unchecked=True),
    )(tok, n_active, src)


def _unpack(xp):
    lo = pltpu.bitcast(xp << 16, F32).astype(BF16)
    hi = pltpu.bitcast(xp & jnp.uint32(0xFFFF0000), F32).astype(BF16)
    return lo, hi


def _expert_up_kernel(te_ref, na_ref, x_ref, wg_ref, wu_ref, bg_ref, bu_ref, o_ref, wgs, wus):
    t = pl.program_id(1)
    fresh = jnp.logical_or(t == 0, te_ref[t] != te_ref[jnp.maximum(t - 1, 0)])

    @pl.when(fresh)
    def _():
        wgs[...] = wg_ref[...].astype(BF16)
        wus[...] = wu_ref[...].astype(BF16)

    @pl.when(t < na_ref[0])
    def _():
        lo, hi = _unpack(x_ref[...])
        half = lo.shape[1]
        g = _dot(lo, wgs[0:half, :]) + _dot(hi, wgs[half:, :]) + bg_ref[...]
        u = _dot(lo, wus[0:half, :]) + _dot(hi, wus[half:, :]) + bu_ref[...]
        g = jnp.minimum(g, SWIGLU_LIMIT)
        u = jnp.clip(u, -SWIGLU_LIMIT, SWIGLU_LIMIT)
        o_ref[...] = (g * _sigmoid(SWIGLU_ALPHA * g) * (u + 1.0)).astype(o_ref.dtype)

    @pl.when(t >= na_ref[0])
    def _():
        o_ref[...] = jnp.zeros(o_ref.shape, o_ref.dtype)


def _expert_up(xs, w_gate, w_up, b_gate, b_up, layer, tile_expert, n_active):
    p, half = xs.shape
    n_exp, d, f = w_gate.shape[1], w_gate.shape[2], w_gate.shape[3]
    tf = _tile(f, 512)
    n_tiles = p // TM
    xrow = lambda fc, t, te, na: (jnp.minimum(t, na[0] - 1), 0)
    wspec = pl.BlockSpec((None, None, d, tf), lambda fc, t, te, na: (layer, te[t], 0, fc))
    bspec = pl.BlockSpec((None, None, 1, tf), lambda fc, t, te, na: (layer, te[t], 0, fc))
    grid_spec = pltpu.PrefetchScalarGridSpec(
        num_scalar_prefetch=2,
        grid=(f // tf, n_tiles),
        in_specs=[pl.BlockSpec((TM, half), xrow), wspec, wspec, bspec, bspec],
        out_specs=pl.BlockSpec((TM, tf), lambda fc, t, te, na: (t, fc)),
        scratch_shapes=[pltpu.VMEM((d, tf), BF16), pltpu.VMEM((d, tf), BF16)],
    )
    n_layers = w_gate.shape[0]
    return pl.pallas_call(
        _expert_up_kernel,
        grid_spec=grid_spec,
        out_shape=jax.ShapeDtypeStruct((p, f), BF16),
        compiler_params=_cparams(2),
    )(tile_expert, n_active, xs, w_gate, w_up,
      b_gate.reshape(n_layers, n_exp, 1, f), b_up.reshape(n_layers, n_exp, 1, f))


def _expert_down_kernel(te_ref, na_ref, h_ref, w_ref, b_ref, o_ref, ws):
    t = pl.program_id(0)
    fresh = jnp.logical_or(t == 0, te_ref[t] != te_ref[jnp.maximum(t - 1, 0)])

    @pl.when(fresh)
    def _():
        ws[...] = w_ref[...].astype(BF16)

    o_ref[...] = _dot(h_ref[...], ws[...]) + b_ref[...]


def _expert_down(hmid, w_down, b_down, layer, tile_expert, n_active):
    p, f = hmid.shape
    n_layers, n_exp, _, d = w_down.shape
    grid_spec = pltpu.PrefetchScalarGridSpec(
        num_scalar_prefetch=2,
        grid=(p // TM,),
        in_specs=[pl.BlockSpec((TM, f), lambda t, te, na: (t, 0)),
                  pl.BlockSpec((None, None, f, d), lambda t, te, na: (layer, te[t], 0, 0)),
                  pl.BlockSpec((None, None, 1, d), lambda t, te, na: (layer, te[t], 0, 0))],
        out_specs=pl.BlockSpec((TM, d), lambda t, te, na: (t, 0)),
        scratch_shapes=[pltpu.VMEM((f, d), BF16)],
    )
    return pl.pallas_call(
        _expert_down_kernel,
        grid_spec=grid_spec,
        out_shape=jax.ShapeDtypeStruct((p, d), F32),
        compiler_params=_cparams(1),
    )(tile_expert, n_active, hmid, w_down, b_down.reshape(n_layers, n_exp, 1, d))


COMBINE_ROWS = 64


def _combine_kernel(pos_ref, y_ref, x_ref, wt_ref, gt_ref, o_ref, buf, sem):
    i = pl.program_id(0)
    tm = x_ref.shape[0]

    def issue(tile, slot):
        def body(r, carry):
            for k in range(TOP_K):
                src = pos_ref[(tile * tm + r) * TOP_K + k]
                pltpu.make_async_copy(y_ref.at[pl.ds(src, 1)], buf.at[slot, k, pl.ds(r, 1)], sem.at[slot]).start()
            return carry
        lax.fori_loop(0, tm, body, 0, unroll=ISSUE_UNROLL // TOP_K)

    @pl.when(i == 0)
    def _():
        issue(0, 0)

    @pl.when(i + 1 < pl.num_programs(0))
    def _():
        issue(i + 1, (i + 1) % 2)

    slot = i % 2
    pltpu.make_async_copy(buf.at[slot], buf.at[slot], sem.at[slot]).wait()
    wt = wt_ref[...]
    acc = wt[:, 0:1] * buf[slot, 0]
    for k in range(1, TOP_K):
        acc = acc + wt[:, k:k + 1] * buf[slot, k]
    d = acc.shape[1]
    a3 = acc.reshape(tm // SUBLANES, SUBLANES, d) * gt_ref[...][None]
    o_ref[...] = x_ref[...] + a3.reshape(tm, d)


def _combine(y, pos, x, wts, gt, rows_per_seq):
    r, d = x.shape
    tm = COMBINE_ROWS
    n_prompt = gt.shape[0] - 1
    tiles_per_seq = rows_per_seq // tm
    grid_spec = pltpu.PrefetchScalarGridSpec(
        num_scalar_prefetch=1,
        grid=(r // tm,),
        in_specs=[pl.BlockSpec(memory_space=pl.ANY),
                  pl.BlockSpec((tm, d), lambda i, pos: (i, 0)),
                  pl.BlockSpec((tm, LANES), lambda i, pos: (i, 0)),
                  pl.BlockSpec((None, SUBLANES, d),
                               lambda i, pos: (jnp.minimum(i // tiles_per_seq, n_prompt), 0, 0))],
        out_specs=pl.BlockSpec((tm, d), lambda i, pos: (i, 0)),
        scratch_shapes=[pltpu.VMEM((2, TOP_K, tm, d), F32), pltpu.SemaphoreType.DMA((2,))],
    )
    return pl.pallas_call(
        _combine_kernel,
        grid_spec=grid_spec,
        out_shape=jax.ShapeDtypeStruct((r, d), F32),
        compiler_params=_cparams(1, unchecked=True),
    )(pos, y, x, wts, gt)


def _dispatch_plan(idx, n_tokens, n_experts, n_rows_max):
    eid = idx[:n_tokens, :TOP_K].reshape(-1)
    n_flat = eid.shape[0]
    order = jnp.argsort(eid, stable=True).astype(I32)
    rank = jnp.argsort(order).astype(I32)
    counts = jnp.sum((eid[:, None] == jnp.arange(n_experts, dtype=I32)[None, :]).astype(I32), axis=0)
    padded = ((counts + TM - 1) // TM) * TM
    ends_p = jnp.cumsum(padded)
    starts_p = ends_p - padded
    starts = jnp.cumsum(counts) - counts
    pos_flat = starts_p[eid] + (rank - starts[eid])
    n_active = (ends_p[-1] // TM).astype(I32)
    tile_start = jnp.arange(n_rows_max // TM, dtype=I32) * TM
    tile_expert = jnp.sum((tile_start[:, None] >= ends_p[None, :]).astype(I32), axis=1)
    tile_expert = jnp.minimum(tile_expert, n_experts - 1)
    last_e = jnp.max(jnp.where(counts > 0, jnp.arange(n_experts, dtype=I32), 0))
    tile_expert = jnp.where(tile_start < ends_p[-1], tile_expert, last_e).astype(I32)
    rows = jnp.arange(n_rows_max, dtype=I32)
    e_row = jnp.repeat(tile_expert, TM)
    off_row = rows - starts_p[e_row]
    src = jnp.clip(starts[e_row] + off_row, 0, n_flat - 1)
    row_token = jnp.where(off_row < counts[e_row], order[src] // TOP_K, 0).astype(I32)
    return row_token, pos_flat.astype(I32), tile_expert, n_active.reshape(1)


def _moe(x, h2p, idx, wts, gt, w_gate, b_gate, w_up, b_up, w_down, b_down, layer, n_tokens, rows_per_seq):
    r = x.shape[0]
    n_experts = w_gate.shape[1]
    n_rows_max = -(-(n_tokens * TOP_K + n_experts * (TM - 1)) // TM) * TM
    row_token, pos_flat, tile_expert, n_active = _dispatch_plan(idx, n_tokens, n_experts, n_rows_max)
    xs = _gather_rows(h2p, row_token, n_active, n_rows_max)
    hmid = _expert_up(xs, w_gate, w_up, b_gate, b_up, layer, tile_expert, n_active)
    y = _expert_down(hmid, w_down, b_down, layer, tile_expert, n_active)
    pos = jnp.zeros((r * TOP_K,), I32).at[:n_tokens * TOP_K].set(pos_flat)
    lane = jnp.arange(LANES)[None, :]
    wts = jnp.where((jnp.arange(r)[:, None] < n_tokens) & (lane < TOP_K), wts, 0.0)
    return _combine(y, pos, x, wts, gt, rows_per_seq)


def kernel(x_prompt, x_sample, c_prompt, c_sample, cache_k, cache_v, cache_logf, page_table, state_delta, state_delta_conv, state_sconv, w_ada, b_ada, g_mix, w_in, w_sconv, w_dconv, a_log, dt_bias, g_dnorm, g_qnorm, g_knorm, b_forget, w_oa, w_ob, w_oc, w_o, g_ffn, w_router, b_router, w_gate, b_gate, w_up, b_up, w_down, b_down):
    bp, seq, d = x_prompt.shape
    nb = x_sample.shape[0]
    assert nb == SUBLANES and x_sample.shape[1] == 1 and bp + nb <= 16
    n_layers = w_ada.shape[0]
    cw = w_sconv.shape[2]
    nh_d, dk, dv = state_delta.shape[2], state_delta.shape[3], state_delta.shape[4]
    nh_f, hd = cache_k.shape[3], cache_k.shape[4]
    page = cache_k.shape[2]
    assert dk == LANES and dv == LANES and hd == LANES and page == LANES
    assert 2 * nh_d + nh_f <= LANES and nh_f <= HEAD_ROWS
    rp = bp * seq
    r = rp + TM
    tiles_per_seq = seq // TM
    n_tokens = rp + nb
    qk_d, v_d = nh_d * dk, nh_d * dv
    f_d = nh_f * hd
    tq = 512 if seq % 512 == 0 else TM

    sizes = (cw, cw, cw, qk_d, qk_d, v_d, v_d, nh_d, nh_d, f_d, f_d, f_d, nh_f, d, d, d)
    off = np.concatenate([[0], np.cumsum(sizes)]).tolist()
    dz_col0 = off[6]

    x = jnp.concatenate([x_prompt.reshape(rp, d), x_sample.reshape(nb, d), jnp.zeros((TM - nb, d), F32)], axis=0)
    c_rows = jnp.concatenate([c_prompt, c_sample, jnp.zeros((16 - bp - nb, d), F32)], axis=0)
    mod_all = _ada(c_rows, w_ada, b_ada)
    logf_hm = jnp.transpose(cache_logf, (0, 1, 3, 2))
    k_hm = jnp.transpose(cache_k, (0, 1, 3, 2, 4))
    w_in_t = jnp.transpose(w_in, (2, 0, 1))
    v_hm = jnp.transpose(cache_v, (0, 1, 3, 2, 4))
    zero_state = jnp.zeros((bp, nh_d, dk, dv), F32)

    def with_sample(prompt_rows, sample_rows):
        pad = jnp.zeros((TM - nb, prompt_rows.shape[1]), prompt_rows.dtype)
        return jnp.concatenate([prompt_rows, sample_rows.astype(prompt_rows.dtype), pad], axis=0)

    def to_chunks(rows, n_rows=DELTA_CHUNK):
        return jnp.pad(rows[:, None, :], ((0, 0), (0, n_rows - 1), (0, 0))).reshape(nb * n_rows, -1)

    def head_rows(a):
        return jnp.pad(a, ((0, 0), (0, HEAD_ROWS - nh_f), (0, 0)))

    outs_p, outs_s = [], []
    for l in range(n_layers):
        mod = mod_all[l]
        modx = jnp.concatenate([jnp.broadcast_to(mod[:bp, None, :], (bp, SUBLANES, 6 * d)),
                                mod[None, bp:bp + nb, :]], axis=0)
        sh_m, sc_m, gt_m, sh_f, sc_f, gt_f = jnp.split(modx, 6, axis=-1)

        h = _norm(x, g_mix[l], sc_m, sh_m, tiles_per_seq)
        z_a = _in_proj(h, w_in_t, l, 0, off[7], False, F32)
        z_f = _in_proj(h, w_in_t, l, off[9], 3 * f_d, False, F32)
        gates = _in_proj(h, w_in_t, l, off[13], 3 * d, True, BF16)
        act = _in_small(h, w_in_t, l, off[7], off[12], a_log[l], dt_bias[l], b_forget[l])
        z_a_s = z_a[rp:rp + nb]
        act_s = act[rp:rp + nb]

        ya_p, sconv_p = _conv_prompt(z_a, w_sconv[l], 0, cw, bp, seq, True, BF16)
        ya_s, sconv_s = _conv_step(z_a_s, w_sconv[l], state_sconv[l], 0, cw, True, BF16)
        taps_a = w_sconv.shape[1]

        qkv_p, dconv_p = _conv_prompt(z_a, w_dconv[l], off[3], 2 * qk_d + v_d, bp, seq, False, F32)
        qkv_s, dconv_s = _conv_step(z_a_s, w_dconv[l], state_delta_conv[l], off[3], 2 * qk_d + v_d, False, F32)
        taps_b = w_dconv.shape[1]
        bb, cb = _delta_prep(act, nh_d)
        yb_p, delta_p = _delta(qkv_p, bb, cb, z_a, dz_col0, zero_state, g_dnorm[l], bp, seq)
        act_s_rows = to_chunks(jnp.where(jnp.arange(LANES)[None, :] < 2 * nh_d, act_s, 0.0))
        bb_s, cb_s = _delta_prep(act_s_rows, nh_d)
        yb_s_chunks, delta_s = _delta(to_chunks(qkv_s), bb_s, cb_s, to_chunks(z_a_s[:, dz_col0:dz_col0 + v_d]), 0,
                                      state_delta[l], g_dnorm[l], nb, DELTA_CHUNK)
        yb_s = yb_s_chunks[::DELTA_CHUNK]

        qn, kn, knb, fcum, k_hm_p, v_hm_p = _qkprep(z_f, act, g_qnorm[l], g_knorm[l], nh_f, bp, tiles_per_seq)
        yc_p = _flash(qn, knb, v_hm_p, fcum, nh_f, 2 * nh_d, bp, seq, tq)
        q_s = qn[rp:rp + nb].astype(F32).reshape(nb, nh_f, hd)
        k_s = kn[rp:rp + nb].reshape(nb, nh_f, hd)
        v_s = z_f[rp:rp + nb, 2 * f_d:].reshape(nb, nh_f, hd)
        logf_s = act_s[:, 2 * nh_d:2 * nh_d + nh_f]
        logf_s_b = jnp.broadcast_to(head_rows(logf_s[:, :, None]), (nb, HEAD_ROWS, page))
        yc_s = _decode_attention(head_rows(q_s), head_rows(k_s), head_rows(v_s), logf_s_b,
                                 logf_hm, k_hm, v_hm, l, page_table)[:, :nh_f].reshape(nb, f_d)

        ya = with_sample(ya_p, ya_s)
        yb = with_sample(yb_p, yb_s)
        yc = with_sample(yc_p, yc_s)
        merged = _merge(ya, yb, yc, w_oa, w_ob, w_oc, l, gates)
        x = _residual_mm(merged, w_o, l, x, gt_m, tiles_per_seq)

        h2p, idx, wts = _norm_router(x, g_ffn[l], sc_f, sh_f, w_router[l], b_router[l], tiles_per_seq)
        x = _moe(x, h2p, idx, wts, gt_f, w_gate, b_gate, w_up, b_up, w_down, b_down, l, n_tokens, seq)

        outs_p.append((jnp.transpose(k_hm_p, (0, 2, 1, 3)),
                       jnp.transpose(v_hm_p, (0, 2, 1, 3)),
                       act[:rp, 2 * nh_d:2 * nh_d + nh_f].reshape(bp, seq, nh_f),
                       delta_p,
                       dconv_p[:, SUBLANES - (taps_b - 1):],
                       sconv_p[:, SUBLANES - (taps_a - 1):]))
        outs_s.append((k_s.reshape(nb, 1, nh_f, hd), v_s.reshape(nb, 1, nh_f, hd), logf_s.reshape(nb, 1, nh_f),
                       delta_s, dconv_s, sconv_s))

    def stack(sts, i):
        return jnp.stack([st[i] for st in sts], axis=0)

    return (x[:rp].reshape(bp, seq, d), x[rp:rp + nb].reshape(nb, 1, d),
            stack(outs_p, 0), stack(outs_p, 1), stack(outs_p, 2), stack(outs_p, 3), stack(outs_p, 4), stack(outs_p, 5),
            stack(outs_s, 0), stack(outs_s, 1), stack(outs_s, 2), stack(outs_s, 3), stack(outs_s, 4), stack(outs_s, 5))
```

```python
import functools

import jax
import jax.numpy as jnp
import numpy as np
from jax import lax
from jax.experimental import pallas as pl
from jax.experimental.pallas import tpu as pltpu

F32 = jnp.float32
BF16 = jnp.bfloat16
I32 = jnp.int32
U32 = jnp.uint32
HI = lax.Precision.HIGHEST

EPS = 1e-6
TOP_K = 4
DELTA_CHUNK = 64
SWIGLU_ALPHA = 1.702
SWIGLU_LIMIT = 7.0
LANES = 128
SUBLANES = 8
HEAD_ROWS = 16
TM = 256
NEG = -1e30
VMEM_LIMIT = 56 * 1024 * 1024


def _cparams(n_axes, vmem=None, unchecked=False):
    return pltpu.CompilerParams(dimension_semantics=("arbitrary",) * n_axes,
                                vmem_limit_bytes=vmem or VMEM_LIMIT,
                                disable_bounds_checks=unchecked)


def _dot(a, b, precision=None):
    return jnp.dot(a, b, preferred_element_type=F32, precision=precision)


def _dot_nt(a, b, precision=None):
    return lax.dot_general(a, b, (((1,), (1,)), ((), ())), preferred_element_type=F32, precision=precision)


def _dot_tn(a, b, precision=None):
    return lax.dot_general(a, b, (((0,), (0,)), ((), ())), preferred_element_type=F32, precision=precision)


def _bdot(a, b):
    return lax.dot_general(a, b, (((2,), (1,)), ((0,), (0,))), preferred_element_type=F32)


def _bdot_nt(a, b):
    return lax.dot_general(a, b, (((2,), (2,)), ((0,), (0,))), preferred_element_type=F32)


def _split3(x):
    p1 = x.astype(BF16)
    r1 = x - p1.astype(F32)
    p2 = r1.astype(BF16)
    p3 = (r1 - p2.astype(F32)).astype(BF16)
    return p1, p2, p3


def _split2(x):
    hi = x.astype(BF16)
    return hi, (x - hi.astype(F32)).astype(BF16)


def _select_dot(x, sel, nt=False):
    d = _dot_nt if nt else _dot
    if nt:
        return sum(d(sel, p) for p in _split3(x))
    return sum(d(p, sel) for p in _split3(x))


def _dot3(a_hi, a_lo, b_hi, b_lo):
    return _dot(a_hi, b_hi) + (_dot(a_hi, b_lo) + _dot(a_lo, b_hi))


def _tile(n, preferred):
    t = preferred
    while n % t:
        t //= 2
    assert t >= LANES, (n, preferred)
    return t


def _iota(shape, dim):
    return lax.broadcasted_iota(I32, shape, dim)


def _sigmoid(x):
    return 1.0 / (1.0 + jnp.exp(-x))


def _ada_kernel(c_ref, w_ref, b_ref, o_ref):
    c = c_ref[...]
    a = (c * _sigmoid(c)).astype(BF16)
    o_ref[...] = _dot(a, w_ref[...].astype(BF16)) + b_ref[...]


def _ada(c_rows, w_ada, b_ada):
    n_layers, d, n = w_ada.shape
    tn = _tile(n, 512)
    return pl.pallas_call(
        _ada_kernel,
        grid=(n_layers, n // tn),
        in_specs=[pl.BlockSpec((16, d), lambda l, j: (0, 0)),
                  pl.BlockSpec((None, d, tn), lambda l, j: (l, 0, j)),
                  pl.BlockSpec((None, 1, tn), lambda l, j: (l, 0, j))],
        out_specs=pl.BlockSpec((None, 16, tn), lambda l, j: (l, 0, j)),
        out_shape=jax.ShapeDtypeStruct((n_layers, 16, n), F32),
        compiler_params=_cparams(2),
    )(c_rows, w_ada, b_ada.reshape(n_layers, 1, n))


def _modulated_norm(x, g, sc, sh):
    ms = jnp.mean(x * x, axis=-1, keepdims=True)
    y = x * lax.rsqrt(ms + EPS) * g
    tm, d = y.shape
    y3 = y.reshape(tm // SUBLANES, SUBLANES, d)
    return (y3 * (1.0 + sc[None]) + sh[None]).reshape(tm, d)


def _norm_kernel(x_ref, g_ref, sc_ref, sh_ref, o_ref):
    o_ref[...] = _modulated_norm(x_ref[...], g_ref[...], sc_ref[...], sh_ref[...]).astype(o_ref.dtype)


def _mod_index(tiles_per_seq, n_prompt):
    return lambda i: (jnp.minimum(i // tiles_per_seq, n_prompt), 0, 0)


def _norm(x, g, sc, sh, tiles_per_seq):
    r, d = x.shape
    n_prompt = sc.shape[0] - 1
    mod_spec = pl.BlockSpec((None, SUBLANES, d), _mod_index(tiles_per_seq, n_prompt))
    return pl.pallas_call(
        _norm_kernel,
        grid=(r // TM,),
        in_specs=[pl.BlockSpec((TM, d), lambda i: (i, 0)),
                  pl.BlockSpec((1, d), lambda i: (0, 0)),
                  mod_spec, mod_spec],
        out_specs=pl.BlockSpec((TM, d), lambda i: (i, 0)),
        out_shape=jax.ShapeDtypeStruct((r, d), BF16),
        compiler_params=_cparams(1),
    )(x, g.reshape(1, d), sc, sh)


def _norm_router_kernel(n_experts, x_ref, g_ref, sc_ref, sh_ref, wr_ref, br_ref, hp_ref, idx_ref, wt_ref):
    h = _modulated_norm(x_ref[...], g_ref[...], sc_ref[...], sh_ref[...])
    tm, d = h.shape
    lo = pltpu.bitcast(h[:, :d // 2].astype(BF16).astype(F32), U32)
    hi = pltpu.bitcast(h[:, d // 2:].astype(BF16).astype(F32), U32)
    hp_ref[...] = (hi & jnp.uint32(0xFFFF0000)) | (lo >> 16)
    logits = _dot(h, wr_ref[...], HI) + br_ref[...]
    lane = _iota(logits.shape, 1)
    work = jnp.where(lane < n_experts, logits, NEG)
    vals, idxs = [], []
    for _ in range(TOP_K):
        m = jnp.max(work, axis=-1, keepdims=True)
        ix = jnp.min(jnp.where(work == m, lane, LANES), axis=-1, keepdims=True)
        vals.append(m)
        idxs.append(ix)
        work = jnp.where(lane == ix, NEG, work)
    es = [jnp.exp(v - vals[0]) for v in vals]
    tot = es[0] + es[1] + es[2] + es[3]
    idx_out = jnp.zeros(logits.shape, I32)
    wt_out = jnp.zeros(logits.shape, F32)
    for k in range(TOP_K):
        idx_out = jnp.where(lane == k, idxs[k], idx_out)
        wt_out = jnp.where(lane == k, es[k] / tot, wt_out)
    idx_ref[...] = idx_out
    wt_ref[...] = wt_out


def _norm_router(x, g, sc, sh, w_router, b_router, tiles_per_seq):
    r, d = x.shape
    n_experts = w_router.shape[1]
    n_prompt = sc.shape[0] - 1
    wr = jnp.pad(w_router, ((0, 0), (0, LANES - n_experts)))
    br = jnp.pad(b_router, (0, LANES - n_experts)).reshape(1, LANES)
    mod_spec = pl.BlockSpec((None, SUBLANES, d), _mod_index(tiles_per_seq, n_prompt))
    return pl.pallas_call(
        functools.partial(_norm_router_kernel, n_experts),
        grid=(r // TM,),
        in_specs=[pl.BlockSpec((TM, d), lambda i: (i, 0)),
                  pl.BlockSpec((1, d), lambda i: (0, 0)),
                  mod_spec, mod_spec,
                  pl.BlockSpec((d, LANES), lambda i: (0, 0)),
                  pl.BlockSpec((1, LANES), lambda i: (0, 0))],
        out_specs=[pl.BlockSpec((TM, d // 2), lambda i: (i, 0)),
                   pl.BlockSpec((TM, LANES), lambda i: (i, 0)),
                   pl.BlockSpec((TM, LANES), lambda i: (i, 0))],
        out_shape=[jax.ShapeDtypeStruct((r, d // 2), U32),
                   jax.ShapeDtypeStruct((r, LANES), I32),
                   jax.ShapeDtypeStruct((r, LANES), F32)],
        compiler_params=_cparams(1),
    )(x, g.reshape(1, d), sc, sh, wr, br)


XPOSE = 512


def _transpose_to_bf16(src, dst):
    for kc in range(dst.shape[0] // XPOSE):
        dst[kc * XPOSE:(kc + 1) * XPOSE, :] = src[:, kc * XPOSE:(kc + 1) * XPOSE].T.astype(BF16)


def _in_proj_kernel(layer, row0, squash, h_ref, wt_ref, o_ref, wbuf, ws, sem):
    j = pl.program_id(0)
    i = pl.program_id(1)
    tn = ws.shape[1]

    def copy(tile, slot):
        return pltpu.make_async_copy(wt_ref.at[pl.ds(row0 + tile * tn, tn), layer], wbuf.at[slot], sem.at[slot])

    @pl.when(i == 0)
    def _():
        @pl.when(j == 0)
        def _():
            copy(0, 0).start()

        @pl.when(j + 1 < pl.num_programs(0))
        def _():
            copy(j + 1, (j + 1) % 2).start()

        slot = j % 2
        copy(j, slot).wait()
        _transpose_to_bf16(wbuf.at[slot], ws)

    y = _dot(h_ref[...], ws[...])
    if squash:
        y = _sigmoid(y)
    o_ref[...] = y.astype(o_ref.dtype)


def _in_proj(h, wt, layer, col0, width, squash, out_dtype):
    r, d = h.shape
    wide = [t for t in (1024, 768, 512) if width % t == 0]
    tn = wide[0] if wide else _tile(width, 1024)
    assert d % XPOSE == 0 or d == XPOSE
    return pl.pallas_call(
        functools.partial(_in_proj_kernel, layer, col0, squash),
        grid=(width // tn, r // TM),
        in_specs=[pl.BlockSpec((TM, d), lambda j, i: (i, 0)),
                  pl.BlockSpec(memory_space=pl.ANY)],
        out_specs=pl.BlockSpec((TM, tn), lambda j, i: (i, j)),
        out_shape=jax.ShapeDtypeStruct((r, width), out_dtype),
        scratch_shapes=[pltpu.VMEM((2, tn, d), F32), pltpu.VMEM((d, tn), BF16), pltpu.SemaphoreType.DMA((2,))],
        compiler_params=_cparams(2),
    )(h, wt)


def _in_small_kernel(layer, row_a, row_b, nh_d, nh_f, h_ref, wt_ref, p_ref, o_ref, wbuf, was, wbs, sem):
    @pl.when(pl.program_id(0) == 0)
    def _():
        ca = pltpu.make_async_copy(wt_ref.at[pl.ds(row_a, LANES), layer], wbuf.at[0], sem.at[0])
        cb = pltpu.make_async_copy(wt_ref.at[pl.ds(row_b, LANES), layer], wbuf.at[1], sem.at[1])
        ca.start()
        cb.start()
        ca.wait()
        cb.wait()
        _transpose_to_bf16(wbuf.at[0], was)
        _transpose_to_bf16(wbuf.at[1], wbs)

    h = h_ref[...]
    lane = _iota((h.shape[0], LANES), 1)
    z = jnp.where(lane < 2 * nh_d, _dot(h, was[...]), _dot(h, wbs[...]))
    beta = _sigmoid(z)
    za = z + p_ref[1:2, :]
    softplus = jnp.maximum(za, 0.0) + jnp.log1p(jnp.exp(-jnp.abs(za)))
    g = -jnp.exp(p_ref[0:1, :]) * softplus
    zf = z + p_ref[2:3, :]
    logf = jnp.minimum(zf, 0.0) - jnp.log1p(jnp.exp(-jnp.abs(zf)))
    o_ref[...] = jnp.where(lane < nh_d, beta, jnp.where(lane < 2 * nh_d, g, logf))


def _in_small(h, wt, layer, col_ba, col_f, a_log, dt_bias, b_forget):
    r, d = h.shape
    nh_d, nh_f = a_log.shape[0], b_forget.shape[0]
    row_b = col_f - 2 * nh_d
    assert col_ba + LANES <= wt.shape[0] and row_b >= 0 and row_b + LANES <= wt.shape[0]
    p = jnp.zeros((SUBLANES, LANES), F32)
    p = p.at[0, nh_d:2 * nh_d].set(a_log).at[1, nh_d:2 * nh_d].set(dt_bias)
    p = p.at[2, 2 * nh_d:2 * nh_d + nh_f].set(b_forget)
    return pl.pallas_call(
        functools.partial(_in_small_kernel, layer, col_ba, row_b, nh_d, nh_f),
        grid=(r // TM,),
        in_specs=[pl.BlockSpec((TM, d), lambda i: (i, 0)),
                  pl.BlockSpec(memory_space=pl.ANY),
                  pl.BlockSpec((SUBLANES, LANES), lambda i: (0, 0))],
        out_specs=pl.BlockSpec((TM, LANES), lambda i: (i, 0)),
        out_shape=jax.ShapeDtypeStruct((r, LANES), F32),
        scratch_shapes=[pltpu.VMEM((2, LANES, d), F32), pltpu.VMEM((d, LANES), BF16),
                        pltpu.VMEM((d, LANES), BF16), pltpu.SemaphoreType.DMA((2,))],
        compiler_params=_cparams(1),
    )(h, wt, p)


def _conv_prompt_kernel(taps, gated, *refs):
    if gated:
        ab_ref, ac_ref, ah_ref, w_ref, y_ref, st_ref, ubuf = refs
    else:
        x_ref, w_ref, y_ref, st_ref, ubuf = refs
    s = pl.program_id(2)
    ts = y_ref.shape[0]

    @pl.when(s == 0)
    def _():
        ubuf[0:SUBLANES, :] = jnp.zeros((SUBLANES, ubuf.shape[1]), F32)

    u = ac_ref[...] * ah_ref[...] if gated else x_ref[...]
    ubuf[SUBLANES:SUBLANES + ts, :] = u
    base = SUBLANES - (taps - 1)
    acc = w_ref[0:1, :] * ubuf[base:base + ts, :]
    for i in range(1, taps):
        acc = acc + w_ref[i:i + 1, :] * ubuf[base + i:base + i + ts, :]
    y = ab_ref[...] * acc if gated else acc * _sigmoid(acc)
    y_ref[...] = y.astype(y_ref.dtype)
    tail = ubuf[ts:ts + SUBLANES, :]
    ubuf[0:SUBLANES, :] = tail

    @pl.when(s == pl.num_programs(2) - 1)
    def _():
        st_ref[...] = tail


def _conv_prompt(z, w, col0, width, n_batch, seq, gated, out_dtype):
    taps = w.shape[0]
    tc = _tile(np.gcd(width, col0) if col0 else width, 512)
    ts = TM
    nt = seq // ts
    grid = (width // tc, n_batch, nt)
    row = lambda j, b, s: b * nt + s
    if gated:
        ins = [pl.BlockSpec((ts, tc), lambda j, b, s, o=o: (row(j, b, s), (col0 + o * width) // tc + j))
               for o in range(3)]
        args = (z, z, z, w)
    else:
        ins = [pl.BlockSpec((ts, tc), lambda j, b, s: (row(j, b, s), col0 // tc + j))]
        args = (z, w)
    ins.append(pl.BlockSpec((taps, tc), lambda j, b, s: (0, j)))
    return pl.pallas_call(
        functools.partial(_conv_prompt_kernel, taps, gated),
        grid=grid,
        in_specs=ins,
        out_specs=[pl.BlockSpec((ts, tc), lambda j, b, s: (row(j, b, s), j)),
                   pl.BlockSpec((None, SUBLANES, tc), lambda j, b, s: (b, 0, j))],
        out_shape=[jax.ShapeDtypeStruct((n_batch * seq, width), out_dtype),
                   jax.ShapeDtypeStruct((n_batch, SUBLANES, width), F32)],
        scratch_shapes=[pltpu.VMEM((ts + SUBLANES, tc), F32)],
        compiler_params=_cparams(3),
    )(*args)


def _conv_step_kernel(taps, gated, *refs):
    if gated:
        ab_ref, ac_ref, ah_ref, w_ref, buf_ref, y_ref, nb_ref = refs
        u = ac_ref[...] * ah_ref[...]
    else:
        x_ref, w_ref, buf_ref, y_ref, nb_ref = refs
        u = x_ref[...]
    acc = w_ref[taps - 1:taps, :] * u
    for i in range(taps - 1):
        acc = acc + w_ref[i:i + 1, :] * buf_ref[i]
    y = ab_ref[...] * acc if gated else acc * _sigmoid(acc)
    y_ref[...] = y.astype(y_ref.dtype)
    for i in range(taps - 2):
        nb_ref[i] = buf_ref[i + 1]
    nb_ref[taps - 2] = u


def _conv_step(z_rows, w, buf, col0, width, gated, out_dtype):
    taps = w.shape[0]
    tc = _tile(np.gcd(width, col0) if col0 else width, 512)
    nb = z_rows.shape[0]
    buf_t = jnp.transpose(buf, (1, 0, 2))
    if gated:
        ins = [pl.BlockSpec((nb, tc), lambda j, o=o: (0, (col0 + o * width) // tc + j)) for o in range(3)]
        args = (z_rows, z_rows, z_rows, w, buf_t)
    else:
        ins = [pl.BlockSpec((nb, tc), lambda j: (0, col0 // tc + j))]
        args = (z_rows, w, buf_t)
    ins += [pl.BlockSpec((taps, tc), lambda j: (0, j)),
            pl.BlockSpec((taps - 1, nb, tc), lambda j: (0, 0, j))]
    y, nbuf = pl.pallas_call(
        functools.partial(_conv_step_kernel, taps, gated),
        grid=(width // tc,),
        in_specs=ins,
        out_specs=[pl.BlockSpec((nb, tc), lambda j: (0, j)),
                   pl.BlockSpec((taps - 1, nb, tc), lambda j: (0, 0, j))],
        out_shape=[jax.ShapeDtypeStruct((nb, width), out_dtype),
                   jax.ShapeDtypeStruct((taps - 1, nb, width), F32)],
        compiler_params=_cparams(1),
    )(*args)
    return y, jnp.transpose(nbuf, (1, 0, 2))


def _delta_prep_kernel(nh, ch, act_ref, bb_ref, cb_ref):
    act = act_ref[...]
    tm = act.shape[0]
    ii = _iota((tm, tm), 0)
    jj = _iota((tm, tm), 1)
    tri = jnp.logical_and(ii >= jj, ii // ch == jj // ch).astype(BF16)
    cum = sum(_dot(tri, p) for p in _split3(act))
    for h in range(nh):
        bb_ref[h] = jnp.broadcast_to(act[:, h:h + 1], (tm, LANES))
        cb_ref[h] = jnp.broadcast_to(cum[:, nh + h:nh + h + 1], (tm, LANES))


def _delta_prep(act, nh):
    r = act.shape[0]
    return pl.pallas_call(
        functools.partial(_delta_prep_kernel, nh, DELTA_CHUNK),
        grid=(r // TM,),
        in_specs=[pl.BlockSpec((TM, LANES), lambda i: (i, 0))],
        out_specs=[pl.BlockSpec((nh, TM, LANES), lambda i: (0, i, 0)),
                   pl.BlockSpec((nh, TM, LANES), lambda i: (0, i, 0))],
        out_shape=[jax.ShapeDtypeStruct((nh, r, LANES), F32),
                   jax.ShapeDtypeStruct((nh, r, LANES), F32)],
        compiler_params=_cparams(1),
    )(act)


def _delta_kernel(hb, q_ref, k_ref, v_ref, bb_ref, cb_ref, dz_ref, s0_ref, gn_ref, o_ref, sf_ref, state):
    c = pl.program_id(2)
    ch = q_ref.shape[0]
    dk = LANES

    @pl.when(c == 0)
    def _():
        state[...] = s0_ref[...]

    def heads(ref):
        return jnp.stack([ref[:, i * dk:(i + 1) * dk] for i in range(hb)], axis=0)

    ii = _iota((ch, LANES), 0)[None]
    jj = _iota((ch, LANES), 1)[None]
    left = jj < ch
    pick0 = jnp.broadcast_to((jj == 0).astype(BF16), (hb, ch, LANES))
    eye_right = (jj - ch == ii).astype(F32)
    zrows = jnp.zeros((hb, ch, LANES), F32)
    zrows_b = jnp.zeros((hb, ch, LANES), BF16)
    n_levels = int(np.log2(ch))

    q = heads(q_ref)
    k = heads(k_ref)
    q = q * lax.rsqrt(jnp.sum(q * q, axis=-1, keepdims=True) + EPS) * (dk ** -0.5)
    k = k * lax.rsqrt(jnp.sum(k * k, axis=-1, keepdims=True) + EPS)
    beta_b = bb_ref[...]
    cum_b = cb_ref[...]
    cum_r = sum(_bdot_nt(pick0, p) for p in _split3(jnp.concatenate([cum_b, zrows], axis=1)))
    diff = cum_b - cum_r
    decay_strict = jnp.exp(jnp.where(jnp.logical_and(left, ii > jj), diff, NEG))
    decay_incl = jnp.exp(jnp.where(jnp.logical_and(left, ii >= jj), diff, NEG))
    kb = k * beta_b
    k_pad = jnp.concatenate([k, zrows], axis=1).astype(BF16)
    la = _bdot_nt(jnp.concatenate([kb, q], axis=1).astype(BF16), k_pad)
    lmat = la[:, :ch] * decay_strict
    attn = la[:, ch:] * decay_incl
    w = jnp.where(left, -lmat, eye_right)
    for _ in range(n_levels):
        w_hi, w_lo = _split2(w)
        p_hi = jnp.where(left, w_hi, jnp.zeros_like(w_hi))
        p_lo = jnp.where(left, w_lo, jnp.zeros_like(w_lo))
        r_hi = jnp.concatenate([w_hi, zrows_b], axis=1)
        r_lo = jnp.concatenate([w_lo, zrows_b], axis=1)
        r = _bdot(p_hi, r_hi) + (_bdot(p_hi, r_lo) + _bdot(p_lo, r_hi))
        w = jnp.where(left, r, w + r)
    t_right = jnp.where(left, 0.0, w).astype(BF16)
    e_cum = jnp.exp(cum_b)
    rhs = jnp.concatenate([heads(v_ref) * beta_b, kb * e_cum], axis=2)
    uw = _bdot(t_right, jnp.concatenate([jnp.zeros_like(rhs), rhs], axis=1).astype(BF16))
    u = uw[:, :, :LANES]
    wk = uw[:, :, LANES:]
    s_prev = state[...]
    sq = _bdot(jnp.concatenate([wk, q * e_cum], axis=1).astype(BF16), s_prev.astype(BF16))
    v_new = u - sq[:, :ch]
    o = sq[:, ch:] + _bdot(attn.astype(BF16), jnp.concatenate([v_new, zrows], axis=1).astype(BF16))
    cum_last = cum_b[:, ch - 1:ch, :]
    k_dec = (k * jnp.exp(cum_last - cum_b)).astype(BF16)
    v_new_b = v_new.astype(BF16)
    s_decayed = s_prev * jnp.exp(cum_last)
    on = o * lax.rsqrt(jnp.mean(o * o, axis=-1, keepdims=True) + EPS) * gn_ref[...][None]
    for i in range(hb):
        state[i] = s_decayed[i] + _dot_tn(k_dec[i], v_new_b[i])
        dz = dz_ref[:, i * dk:(i + 1) * dk]
        o_ref[:, i * dk:(i + 1) * dk] = (on[i] * (dz * _sigmoid(dz))).astype(o_ref.dtype)

    @pl.when(c == pl.num_programs(2) - 1)
    def _():
        sf_ref[...] = state[...]


def _delta(qkv, bb, cb, dz_arr, dz_col0, s0, g_dnorm, n_batch, seq):
    nh, dk, dv = s0.shape[1], s0.shape[2], s0.shape[3]
    ch = DELTA_CHUNK
    nc = seq // ch
    hb = max(c for c in (nh, 4, 2, 1) if nh % c == 0 and dz_col0 % (c * dk) == 0)
    wb = hb * dk
    ng = nh // hb
    row = lambda b, g, c: b * nc + c
    return pl.pallas_call(
        functools.partial(_delta_kernel, hb),
        grid=(n_batch, ng, nc),
        in_specs=[pl.BlockSpec((ch, wb), lambda b, g, c: (row(b, g, c), g)),
                  pl.BlockSpec((ch, wb), lambda b, g, c: (row(b, g, c), ng + g)),
                  pl.BlockSpec((ch, wb), lambda b, g, c: (row(b, g, c), 2 * ng + g)),
                  pl.BlockSpec((hb, ch, LANES), lambda b, g, c: (g, row(b, g, c), 0)),
                  pl.BlockSpec((hb, ch, LANES), lambda b, g, c: (g, row(b, g, c), 0)),
                  pl.BlockSpec((ch, wb), lambda b, g, c: (row(b, g, c), dz_col0 // wb + g)),
                  pl.BlockSpec((None, hb, dk, dv), lambda b, g, c: (b, g, 0, 0)),
                  pl.BlockSpec((1, dv), lambda b, g, c: (0, 0))],
        out_specs=[pl.BlockSpec((ch, wb), lambda b, g, c: (row(b, g, c), g)),
                   pl.BlockSpec((None, hb, dk, dv), lambda b, g, c: (b, g, 0, 0))],
        out_shape=[jax.ShapeDtypeStruct((n_batch * seq, nh * dv), BF16),
                   jax.ShapeDtypeStruct(s0.shape, F32)],
        scratch_shapes=[pltpu.VMEM((hb, dk, dv), F32)],
        compiler_params=_cparams(3),
    )(qkv, qkv, qkv, bb, cb, dz_arr, s0, g_dnorm.reshape(1, dv))


def _qkprep_kernel(nh, hd, q_ref, k_ref, v_ref, act_ref, gq_ref, gk_ref,
                   qn_ref, kn_ref, knb_ref, fc_ref, kh_ref, vh_ref, carry):
    t = pl.program_id(0)
    scale = hd ** -0.5
    for h in range(nh):
        sl = slice(h * hd, (h + 1) * hd)
        q = q_ref[:, sl]
        qn = q * lax.rsqrt(jnp.mean(q * q, axis=-1, keepdims=True) + EPS) * gq_ref[...]
        qn_ref[:, sl] = (qn * scale).astype(BF16)
        k = k_ref[:, sl]
        kn = k * lax.rsqrt(jnp.mean(k * k, axis=-1, keepdims=True) + EPS) * gk_ref[...]
        kn_ref[:, sl] = kn
        knb_ref[:, sl] = kn.astype(BF16)
        kh_ref[h] = kn
        vh_ref[h] = v_ref[:, sl]

    tm = act_ref.shape[0]
    tri = (_iota((tm, tm), 0) >= _iota((tm, tm), 1)).astype(BF16)
    fc = sum(_dot(tri, p) for p in _split3(act_ref[...])) + carry[0:1, :]
    fc_ref[...] = fc
    carry[...] = jnp.broadcast_to(fc[tm - 1:tm, :], carry.shape)


def _qkprep(z_f, act, g_q, g_k, nh, n_prompt, tiles_per_seq):
    r = z_f.shape[0]
    hd = g_q.shape[0]
    w = nh * hd
    n_tiles = r // TM
    n_ptiles = n_prompt * tiles_per_seq
    assert n_tiles == n_ptiles + 1
    rt = lambda t: jnp.where(t == 0, n_ptiles, t - 1)
    pt = lambda t: jnp.maximum(t - 1, 0)
    rows = lambda c: pl.BlockSpec((TM, w), lambda t: (rt(t), c))
    head_major = pl.BlockSpec((None, nh, TM, hd), lambda t: (pt(t) // tiles_per_seq, 0, pt(t) % tiles_per_seq, 0))

    def kern(*refs):
        carry = refs[-1]

        t = pl.program_id(0)

        @pl.when(jnp.logical_or(t == 0, (t - 1) % tiles_per_seq == 0))
        def _():
            carry[...] = jnp.zeros(carry.shape, F32)
        _qkprep_kernel(nh, hd, *refs)

    return pl.pallas_call(
        kern,
        grid=(n_tiles,),
        in_specs=[rows(0), rows(1), rows(2),
                  pl.BlockSpec((TM, LANES), lambda t: (rt(t), 0)),
                  pl.BlockSpec((1, hd), lambda t: (0, 0)),
                  pl.BlockSpec((1, hd), lambda t: (0, 0))],
        out_specs=[rows(0), rows(0), rows(0),
                   pl.BlockSpec((TM, LANES), lambda t: (rt(t), 0)),
                   head_major, head_major],
        out_shape=[jax.ShapeDtypeStruct((r, w), BF16),
                   jax.ShapeDtypeStruct((r, w), F32),
                   jax.ShapeDtypeStruct((r, w), BF16),
                   jax.ShapeDtypeStruct((r, LANES), F32),
                   jax.ShapeDtypeStruct((n_prompt, nh, tiles_per_seq * TM, hd), F32),
                   jax.ShapeDtypeStruct((n_prompt, nh, tiles_per_seq * TM, hd), F32)],
        scratch_shapes=[pltpu.VMEM((SUBLANES, LANES), F32)],
        compiler_params=_cparams(1),
    )(z_f, z_f, z_f, act, g_q.reshape(1, hd), g_k.reshape(1, hd))


def _flash_kernel(lane0, hb, qi_ref, ki_ref, q_ref, k_ref, v_ref, fq_ref, fk_ref, o_ref,
                  m_ref, l_ref, acc_ref, fqb_ref):
    head0 = lane0 + pl.program_id(1) * hb
    qi = qi_ref[pl.program_id(2)]
    ki = ki_ref[pl.program_id(2)]
    tq = q_ref.shape[0]
    tk = k_ref.shape[0]
    hd = q_ref.shape[1] // hb

    def heads(ref):
        return jnp.stack([ref[:, i * hd:(i + 1) * hd] for i in range(hb)], axis=0)

    @pl.when(ki == 0)
    def _():
        m_ref[...] = jnp.full(m_ref.shape, NEG, F32)
        l_ref[...] = jnp.zeros(l_ref.shape, F32)
        acc_ref[...] = jnp.zeros(acc_ref.shape, F32)
        parts = _split3(fq_ref[...])
        for i in range(hb):
            sel = (_iota((LANES, LANES), 0) == head0 + i).astype(BF16)
            fqb_ref[i] = sum(_dot(p, sel) for p in parts)

    def update(on_diagonal):
        s = _bdot_nt(heads(q_ref), heads(k_ref))
        pick = (_iota((SUBLANES, LANES), 1) == head0 + _iota((SUBLANES, LANES), 0)).astype(BF16)
        fk_rows = _select_dot(fk_ref[...], pick, nt=True)
        fk = jnp.stack([fk_rows[i:i + 1, :] for i in range(hb)], axis=0)
        s = s + jnp.concatenate([fqb_ref[...]] * (tk // LANES), axis=2) - fk
        if on_diagonal:
            s = jnp.where((_iota((tq, tk), 1) <= _iota((tq, tk), 0))[None], s, NEG)
        m_prev = m_ref[...]
        m_new = jnp.maximum(m_prev, jnp.max(s, axis=-1, keepdims=True))
        alpha = jnp.exp(m_prev - m_new)
        p = jnp.exp(s - m_new)
        l_ref[...] = alpha * l_ref[...] + jnp.sum(p, axis=-1, keepdims=True)
        acc_ref[...] = alpha * acc_ref[...] + _bdot(p.astype(BF16), v_ref[...].astype(BF16))
        m_ref[...] = m_new

    @pl.when(ki < qi)
    def _():
        update(False)

    @pl.when(ki == qi)
    def _():
        update(True)
        out = acc_ref[...] / l_ref[...]
        for i in range(hb):
            o_ref[:, i * hd:(i + 1) * hd] = out[i].astype(o_ref.dtype)


def _flash(qn, knb, v_hm, fcum, nh, lane0, n_batch, seq, tq):
    hd = qn.shape[1] // nh
    nq = seq // tq
    pairs = [(qi, ki) for qi in range(nq) for ki in range(qi + 1)]
    qi_arr = jnp.asarray([p[0] for p in pairs], I32)
    ki_arr = jnp.asarray([p[1] for p in pairs], I32)
    hb = max(c for c in (4, 2, 1) if nh % c == 0)
    assert hb <= SUBLANES
    qrow = lambda b, g, p, qa, ka: b * nq + qa[p]
    krow = lambda b, g, p, qa, ka: b * nq + ka[p]
    grid_spec = pltpu.PrefetchScalarGridSpec(
        num_scalar_prefetch=2,
        grid=(n_batch, nh // hb, len(pairs)),
        in_specs=[pl.BlockSpec((tq, hb * hd), lambda b, g, p, qa, ka: (qrow(b, g, p, qa, ka), g)),
                  pl.BlockSpec((tq, hb * hd), lambda b, g, p, qa, ka: (krow(b, g, p, qa, ka), g)),
                  pl.BlockSpec((None, hb, tq, hd), lambda b, g, p, qa, ka: (b, g, ka[p], 0)),
                  pl.BlockSpec((tq, LANES), lambda b, g, p, qa, ka: (qrow(b, g, p, qa, ka), 0)),
                  pl.BlockSpec((tq, LANES), lambda b, g, p, qa, ka: (krow(b, g, p, qa, ka), 0))],
        out_specs=pl.BlockSpec((tq, hb * hd), lambda b, g, p, qa, ka: (qrow(b, g, p, qa, ka), g)),
        scratch_shapes=[pltpu.VMEM((hb, tq, 1), F32), pltpu.VMEM((hb, tq, 1), F32),
                        pltpu.VMEM((hb, tq, hd), F32), pltpu.VMEM((hb, tq, LANES), F32)],
    )
    return pl.pallas_call(
        functools.partial(_flash_kernel, lane0, hb),
        grid_spec=grid_spec,
        out_shape=jax.ShapeDtypeStruct((n_batch * seq, nh * hd), BF16),
        compiler_params=_cparams(3),
    )(qi_arr, ki_arr, qn, knb, v_hm, fcum, fcum)


def _decode_kernel(nh, pt_ref, q_ref, kn_ref, vn_ref, lfn_ref, lfa_ref, lfb_ref, ka_ref, kb_ref, va_ref, vb_ref,
                   o_ref, m_ref, l_ref, acc_ref, carry, xpad):
    j = pl.program_id(1)
    page = carry.shape[1]
    hrow = _iota((HEAD_ROWS, LANES), 0)

    @pl.when(j == 0)
    def _():
        m_ref[...] = jnp.full(m_ref.shape, NEG, F32)
        l_ref[...] = jnp.zeros(l_ref.shape, F32)
        acc_ref[...] = jnp.zeros(acc_ref.shape, F32)
        carry[...] = jnp.zeros(carry.shape, F32)
        xpad[...] = jnp.zeros(xpad.shape, F32)

    after = (_iota((page, page), 0) > _iota((page, page), 1)).astype(BF16)
    ones = jnp.ones((page, page), BF16)
    qb = q_ref[...].astype(BF16)

    def page_update(lf_ref, k_ref, v_ref):
        xpad[0:nh, :] = lf_ref[...]
        parts = _split3(xpad[...])
        s = sum(_dot(p, after) for p in parts) + carry[...] + lfn_ref[...]
        carry[...] = carry[...] + sum(_dot(p, ones) for p in parts)
        s_all = _dot_nt(qb, k_ref[...].reshape(nh * page, qb.shape[1]).astype(BF16))
        for h in range(nh):
            s = s + jnp.where(hrow == h, s_all[:, h * page:(h + 1) * page], 0.0)
        m_prev = m_ref[...]
        m_new = jnp.maximum(m_prev, jnp.max(s, axis=-1, keepdims=True))
        alpha = jnp.exp(m_prev - m_new)
        p = jnp.exp(s - m_new)
        l_ref[...] = alpha * l_ref[...] + jnp.sum(p, axis=-1, keepdims=True)
        p_blocks = jnp.concatenate([jnp.where(hrow == h, p, 0.0) for h in range(nh)], axis=1).astype(BF16)
        pv = _dot(p_blocks, v_ref[...].reshape(nh * page, qb.shape[1]).astype(BF16))
        acc_ref[...] = alpha * acc_ref[...] + pv
        m_ref[...] = m_new

    page_update(lfb_ref, kb_ref, vb_ref)
    page_update(lfa_ref, ka_ref, va_ref)

    @pl.when(j == pl.num_programs(1) - 1)
    def _():
        s_new = jnp.sum(q_ref[...] * kn_ref[...], axis=-1, keepdims=True)
        m_prev = m_ref[...]
        m_new = jnp.maximum(m_prev, s_new)
        alpha = jnp.exp(m_prev - m_new)
        pn = jnp.exp(s_new - m_new)
        l_fin = alpha * l_ref[...] + pn
        o_ref[...] = (alpha * acc_ref[...] + pn * vn_ref[...]) / l_fin


def _decode_attention(q, k_new, v_new, logf_new_b, logf_hm, k_hm, v_hm, layer, page_table):
    nb, _, hd = q.shape
    nh = k_hm.shape[2]
    npg = page_table.shape[1]
    page = k_hm.shape[3]
    assert npg % 2 == 0 and page == LANES and hd == LANES
    half = npg // 2
    small = pl.BlockSpec((None, HEAD_ROWS, hd), lambda b, j, pt: (b, 0, 0))
    pa = lambda b, j, pt: pt[b, npg - 2 - 2 * j]
    pb = lambda b, j, pt: pt[b, npg - 1 - 2 * j]
    lf = lambda pg: pl.BlockSpec((None, None, nh, page), lambda b, j, pt: (layer, pg(b, j, pt), 0, 0))
    cache = lambda pg: pl.BlockSpec((None, None, nh, page, hd), lambda b, j, pt: (layer, pg(b, j, pt), 0, 0, 0))
    grid_spec = pltpu.PrefetchScalarGridSpec(
        num_scalar_prefetch=1,
        grid=(nb, half),
        in_specs=[small, small, small,
                  pl.BlockSpec((None, HEAD_ROWS, page), lambda b, j, pt: (b, 0, 0)),
                  lf(pa), lf(pb), cache(pa), cache(pb), cache(pa), cache(pb)],
        out_specs=pl.BlockSpec((None, HEAD_ROWS, hd), lambda b, j, pt: (b, 0, 0)),
        scratch_shapes=[pltpu.VMEM((HEAD_ROWS, 1), F32), pltpu.VMEM((HEAD_ROWS, 1), F32),
                        pltpu.VMEM((HEAD_ROWS, hd), F32), pltpu.VMEM((HEAD_ROWS, page), F32),
                        pltpu.VMEM((HEAD_ROWS, page), F32)],
    )
    return pl.pallas_call(
        functools.partial(_decode_kernel, nh),
        grid_spec=grid_spec,
        out_shape=jax.ShapeDtypeStruct((nb, HEAD_ROWS, hd), F32),
        compiler_params=_cparams(2),
    )(page_table, q, k_new, v_new, logf_new_b, logf_hm, logf_hm, k_hm, k_hm, v_hm, v_hm)


def _merge_kernel(ya_ref, yb_ref, yc_ref, wa_ref, wb_ref, wc_ref, ga_ref, gb_ref, gc_ref, o_ref,
                  was, wbs, wcs):
    @pl.when(pl.program_id(1) == 0)
    def _():
        was[...] = wa_ref[...].astype(BF16)
        wbs[...] = wb_ref[...].astype(BF16)
        wcs[...] = wc_ref[...].astype(BF16)

    acc = ga_ref[...].astype(F32) * _dot(ya_ref[...], was[...])
    acc = acc + gb_ref[...].astype(F32) * _dot(yb_ref[...], wbs[...])
    acc = acc + gc_ref[...].astype(F32) * _dot(yc_ref[...], wcs[...])
    o_ref[...] = acc.astype(o_ref.dtype)


def _merge(ya, yb, yc, w_oa, w_ob, w_oc, layer, gates):
    r = ya.shape[0]
    ka, kb, kc = ya.shape[1], yb.shape[1], yc.shape[1]
    d = w_oa.shape[2]
    tn = _tile(d, 1024)
    nj = d // tn
    wspec = lambda k: pl.BlockSpec((None, k, tn), lambda j, i: (layer, 0, j))
    gspec = lambda o: pl.BlockSpec((TM, tn), lambda j, i: (i, o * nj + j))
    return pl.pallas_call(
        _merge_kernel,
        grid=(nj, r // TM),
        in_specs=[pl.BlockSpec((TM, ka), lambda j, i: (i, 0)),
                  pl.BlockSpec((TM, kb), lambda j, i: (i, 0)),
                  pl.BlockSpec((TM, kc), lambda j, i: (i, 0)),
                  wspec(ka), wspec(kb), wspec(kc), gspec(0), gspec(1), gspec(2)],
        out_specs=pl.BlockSpec((TM, tn), lambda j, i: (i, j)),
        out_shape=jax.ShapeDtypeStruct((r, d), BF16),
        scratch_shapes=[pltpu.VMEM((ka, tn), BF16), pltpu.VMEM((kb, tn), BF16), pltpu.VMEM((kc, tn), BF16)],
        compiler_params=_cparams(2),
    )(ya, yb, yc, w_oa, w_ob, w_oc, gates, gates, gates)


def _residual_mm_kernel(m_ref, w_ref, x_ref, gt_ref, o_ref, ws):
    @pl.when(pl.program_id(1) == 0)
    def _():
        ws[...] = w_ref[...].astype(BF16)

    y = _dot(m_ref[...], ws[...])
    tm, tn = y.shape
    y3 = y.reshape(tm // SUBLANES, SUBLANES, tn) * gt_ref[...][None]
    o_ref[...] = x_ref[...] + y3.reshape(tm, tn)


def _residual_mm(m, w_o, layer, x, gt, tiles_per_seq):
    r, k = m.shape
    d = w_o.shape[2]
    tn = _tile(d, 1024)
    n_prompt = gt.shape[0] - 1
    return pl.pallas_call(
        _residual_mm_kernel,
        grid=(d // tn, r // TM),
        in_specs=[pl.BlockSpec((TM, k), lambda j, i: (i, 0)),
                  pl.BlockSpec((None, k, tn), lambda j, i: (layer, 0, j)),
                  pl.BlockSpec((TM, tn), lambda j, i: (i, j)),
                  pl.BlockSpec((None, SUBLANES, tn),
                               lambda j, i: (jnp.minimum(i // tiles_per_seq, n_prompt), 0, j))],
        out_specs=pl.BlockSpec((TM, tn), lambda j, i: (i, j)),
        out_shape=jax.ShapeDtypeStruct((r, d), F32),
        scratch_shapes=[pltpu.VMEM((k, tn), BF16)],
        compiler_params=_cparams(2),
    )(m, w_o, x, gt)


ISSUE_UNROLL = 8


def _gather_rows_kernel(tok_ref, na_ref, src_ref, o_ref, buf, sem):
    t = pl.program_id(0)
    n_act = na_ref[0]
    tm = o_ref.shape[0]

    def copy(tile, r, slot):
        return pltpu.make_async_copy(src_ref.at[pl.ds(tok_ref[tile * tm + r], 1)],
                                     buf.at[slot, pl.ds(r, 1)], sem.at[slot])

    def issue(tile, slot):
        def body(r, carry):
            copy(tile, r, slot).start()
            return carry
        lax.fori_loop(0, tm, body, 0, unroll=ISSUE_UNROLL)

    @pl.when(jnp.logical_and(t == 0, n_act > 0))
    def _():
        issue(0, 0)

    @pl.when(t + 1 < n_act)
    def _():
        issue(t + 1, (t + 1) % 2)

    @pl.when(t < n_act)
    def _():
        slot = t % 2
        pltpu.make_async_copy(buf.at[slot], buf.at[slot], sem.at[slot]).wait()
        o_ref[...] = buf[slot]

    @pl.when(t >= n_act)
    def _():
        o_ref[...] = jnp.zeros(o_ref.shape, o_ref.dtype)


def _gather_rows(src, tok, n_active, n_out):
    w = src.shape[1]
    grid_spec = pltpu.PrefetchScalarGridSpec(
        num_scalar_prefetch=2,
        grid=(n_out // TM,),
        in_specs=[pl.BlockSpec(memory_space=pl.ANY)],
        out_specs=pl.BlockSpec((TM, w), lambda t, tok, na: (t, 0)),
        scratch_shapes=[pltpu.VMEM((2, TM, w), src.dtype), pltpu.SemaphoreType.DMA((2,))],
    )
    return pl.pallas_call(
        _gather_rows_kernel,
        grid_spec=grid_spec,
        out_shape=jax.ShapeDtypeStruct((n_out, w), src.dtype),
        compiler_params=_cparams(1, unchecked=True),
    )(tok, n_active, src)


def _unpack(xp):
    lo = pltpu.bitcast(xp << 16, F32).astype(BF16)
    hi = pltpu.bitcast(xp & jnp.uint32(0xFFFF0000), F32).astype(BF16)
    return lo, hi


def _expert_up_kernel(te_ref, na_ref, x_ref, wg_ref, wu_ref, bg_ref, bu_ref, o_ref, wgs, wus):
    t = pl.program_id(1)
    fresh = jnp.logical_or(t == 0, te_ref[t] != te_ref[jnp.maximum(t - 1, 0)])

    @pl.when(fresh)
    def _():
        wgs[...] = wg_ref[...].astype(BF16)
        wus[...] = wu_ref[...].astype(BF16)

    @pl.when(t < na_ref[0])
    def _():
        lo, hi = _unpack(x_ref[...])
        half = lo.shape[1]
        g = _dot(lo, wgs[0:half, :]) + _dot(hi, wgs[half:, :]) + bg_ref[...]
        u = _dot(lo, wus[0:half, :]) + _dot(hi, wus[half:, :]) + bu_ref[...]
        g = jnp.minimum(g, SWIGLU_LIMIT)
        u = jnp.clip(u, -SWIGLU_LIMIT, SWIGLU_LIMIT)
        o_ref[...] = (g * _sigmoid(SWIGLU_ALPHA * g) * (u + 1.0)).astype(o_ref.dtype)

    @pl.when(t >= na_ref[0])
    def _():
        o_ref[...] = jnp.zeros(o_ref.shape, o_ref.dtype)


def _expert_up(xs, w_gate, w_up, b_gate, b_up, layer, tile_expert, n_active):
    p, half = xs.shape
    n_exp, d, f = w_gate.shape[1], w_gate.shape[2], w_gate.shape[3]
    tf = _tile(f, 512)
    n_tiles = p // TM
    xrow = lambda fc, t, te, na: (jnp.minimum(t, na[0] - 1), 0)
    wspec = pl.BlockSpec((None, None, d, tf), lambda fc, t, te, na: (layer, te[t], 0, fc))
    bspec = pl.BlockSpec((None, None, 1, tf), lambda fc, t, te, na: (layer, te[t], 0, fc))
    grid_spec = pltpu.PrefetchScalarGridSpec(
        num_scalar_prefetch=2,
        grid=(f // tf, n_tiles),
        in_specs=[pl.BlockSpec((TM, half), xrow), wspec, wspec, bspec, bspec],
        out_specs=pl.BlockSpec((TM, tf), lambda fc, t, te, na: (t, fc)),
        scratch_shapes=[pltpu.VMEM((d, tf), BF16), pltpu.VMEM((d, tf), BF16)],
    )
    n_layers = w_gate.shape[0]
    return pl.pallas_call(
        _expert_up_kernel,
        grid_spec=grid_spec,
        out_shape=jax.ShapeDtypeStruct((p, f), BF16),
        compiler_params=_cparams(2),
    )(tile_expert, n_active, xs, w_gate, w_up,
      b_gate.reshape(n_layers, n_exp, 1, f), b_up.reshape(n_layers, n_exp, 1, f))


def _expert_down_kernel(te_ref, na_ref, h_ref, w_ref, b_ref, o_ref, ws):
    t = pl.program_id(0)
    fresh = jnp.logical_or(t == 0, te_ref[t] != te_ref[jnp.maximum(t - 1, 0)])

    @pl.when(fresh)
    def _():
        ws[...] = w_ref[...].astype(BF16)

    o_ref[...] = _dot(h_ref[...], ws[...]) + b_ref[...]


def _expert_down(hmid, w_down, b_down, layer, tile_expert, n_active):
    p, f = hmid.shape
    n_layers, n_exp, _, d = w_down.shape
    grid_spec = pltpu.PrefetchScalarGridSpec(
        num_scalar_prefetch=2,
        grid=(p // TM,),
        in_specs=[pl.BlockSpec((TM, f), lambda t, te, na: (t, 0)),
                  pl.BlockSpec((None, None, f, d), lambda t, te, na: (layer, te[t], 0, 0)),
                  pl.BlockSpec((None, None, 1, d), lambda t, te, na: (layer, te[t], 0, 0))],
        out_specs=pl.BlockSpec((TM, d), lambda t, te, na: (t, 0)),
        scratch_shapes=[pltpu.VMEM((f, d), BF16)],
    )
    return pl.pallas_call(
        _expert_down_kernel,
        grid_spec=grid_spec,
        out_shape=jax.ShapeDtypeStruct((p, d), F32),
        compiler_params=_cparams(1),
    )(tile_expert, n_active, hmid, w_down, b_down.reshape(n_layers, n_exp, 1, d))


COMBINE_ROWS = 64


def _combine_kernel(pos_ref, y_ref, x_ref, wt_ref, gt_ref, o_ref, buf, sem):
    i = pl.program_id(0)
    tm = x_ref.shape[0]

    def issue(tile, slot):
        def body(r, carry):
            for k in range(TOP_K):
                src = pos_ref[(tile * tm + r) * TOP_K + k]
                pltpu.make_async_copy(y_ref.at[pl.ds(src, 1)], buf.at[slot, k, pl.ds(r, 1)], sem.at[slot]).start()
            return carry
        lax.fori_loop(0, tm, body, 0, unroll=ISSUE_UNROLL // TOP_K)

    @pl.when(i == 0)
    def _():
        issue(0, 0)

    @pl.when(i + 1 < pl.num_programs(0))
    def _():
        issue(i + 1, (i + 1) % 2)

    slot = i % 2
    pltpu.make_async_copy(buf.at[slot], buf.at[slot], sem.at[slot]).wait()
    wt = wt_ref[...]
    acc = wt[:, 0:1] * buf[slot, 0]
    for k in range(1, TOP_K):
        acc = acc + wt[:, k:k + 1] * buf[slot, k]
    d = acc.shape[1]
    a3 = acc.reshape(tm // SUBLANES, SUBLANES, d) * gt_ref[...][None]
    o_ref[...] = x_ref[...] + a3.reshape(tm, d)


def _combine(y, pos, x, wts, gt, rows_per_seq):
    r, d = x.shape
    tm = COMBINE_ROWS
    n_prompt = gt.shape[0] - 1
    tiles_per_seq = rows_per_seq // tm
    grid_spec = pltpu.PrefetchScalarGridSpec(
        num_scalar_prefetch=1,
        grid=(r // tm,),
        in_specs=[pl.BlockSpec(memory_space=pl.ANY),
                  pl.BlockSpec((tm, d), lambda i, pos: (i, 0)),
                  pl.BlockSpec((tm, LANES), lambda i, pos: (i, 0)),
                  pl.BlockSpec((None, SUBLANES, d),
                               lambda i, pos: (jnp.minimum(i // tiles_per_seq, n_prompt), 0, 0))],
        out_specs=pl.BlockSpec((tm, d), lambda i, pos: (i, 0)),
        scratch_shapes=[pltpu.VMEM((2, TOP_K, tm, d), F32), pltpu.SemaphoreType.DMA((2,))],
    )
    return pl.pallas_call(
        _combine_kernel,
        grid_spec=grid_spec,
        out_shape=jax.ShapeDtypeStruct((r, d), F32),
        compiler_params=_cparams(1, unchecked=True),
    )(pos, y, x, wts, gt)


def _dispatch_plan(idx, n_tokens, n_experts, n_rows_max):
    eid = idx[:n_tokens, :TOP_K].reshape(-1)
    n_flat = eid.shape[0]
    order = jnp.argsort(eid, stable=True).astype(I32)
    rank = jnp.argsort(order).astype(I32)
    counts = jnp.sum((eid[:, None] == jnp.arange(n_experts, dtype=I32)[None, :]).astype(I32), axis=0)
    padded = ((counts + TM - 1) // TM) * TM
    ends_p = jnp.cumsum(padded)
    starts_p = ends_p - padded
    starts = jnp.cumsum(counts) - counts
    pos_flat = starts_p[eid] + (rank - starts[eid])
    n_active = (ends_p[-1] // TM).astype(I32)
    tile_start = jnp.arange(n_rows_max // TM, dtype=I32) * TM
    tile_expert = jnp.sum((tile_start[:, None] >= ends_p[None, :]).astype(I32), axis=1)
    tile_expert = jnp.minimum(tile_expert, n_experts - 1)
    last_e = jnp.max(jnp.where(counts > 0, jnp.arange(n_experts, dtype=I32), 0))
    tile_expert = jnp.where(tile_start < ends_p[-1], tile_expert, last_e).astype(I32)
    rows = jnp.arange(n_rows_max, dtype=I32)
    e_row = jnp.repeat(tile_expert, TM)
    off_row = rows - starts_p[e_row]
    src = jnp.clip(starts[e_row] + off_row, 0, n_flat - 1)
    row_token = jnp.where(off_row < counts[e_row], order[src] // TOP_K, 0).astype(I32)
    return row_token, pos_flat.astype(I32), tile_expert, n_active.reshape(1)


def _moe(x, h2p, idx, wts, gt, w_gate, b_gate, w_up, b_up, w_down, b_down, layer, n_tokens, rows_per_seq):
    r = x.shape[0]
    n_experts = w_gate.shape[1]
    n_rows_max = -(-(n_tokens * TOP_K + n_experts * (TM - 1)) // TM) * TM
    row_token, pos_flat, tile_expert, n_active = _dispatch_plan(idx, n_tokens, n_experts, n_rows_max)
    xs = _gather_rows(h2p, row_token, n_active, n_rows_max)
    hmid = _expert_up(xs, w_gate, w_up, b_gate, b_up, layer, tile_expert, n_active)
    y = _expert_down(hmid, w_down, b_down, layer, tile_expert, n_active)
    pos = jnp.zeros((r * TOP_K,), I32).at[:n_tokens * TOP_K].set(pos_flat)
    lane = jnp.arange(LANES)[None, :]
    wts = jnp.where((jnp.arange(r)[:, None] < n_tokens) & (lane < TOP_K), wts, 0.0)
    return _combine(y, pos, x, wts, gt, rows_per_seq)


def kernel(x_prompt, x_sample, c_prompt, c_sample, cache_k, cache_v, cache_logf, page_table, state_delta, state_delta_conv, state_sconv, w_ada, b_ada, g_mix, w_in, w_sconv, w_dconv, a_log, dt_bias, g_dnorm, g_qnorm, g_knorm, b_forget, w_oa, w_ob, w_oc, w_o, g_ffn, w_router, b_router, w_gate, b_gate, w_up, b_up, w_down, b_down):
    bp, seq, d = x_prompt.shape
    nb = x_sample.shape[0]
    assert nb == SUBLANES and x_sample.shape[1] == 1 and bp + nb <= 16
    n_layers = w_ada.shape[0]
    cw = w_sconv.shape[2]
    nh_d, dk, dv = state_delta.shape[2], state_delta.shape[3], state_delta.shape[4]
    nh_f, hd = cache_k.shape[3], cache_k.shape[4]
    page = cache_k.shape[2]
    assert dk == LANES and dv == LANES and hd == LANES and page == LANES
    assert 2 * nh_d + nh_f <= LANES and nh_f <= HEAD_ROWS
    rp = bp * seq
    r = rp + TM
    tiles_per_seq = seq // TM
    n_tokens = rp + nb
    qk_d, v_d = nh_d * dk, nh_d * dv
    f_d = nh_f * hd
    tq = 512 if seq % 512 == 0 else TM

    sizes = (cw, cw, cw, qk_d, qk_d, v_d, v_d, nh_d, nh_d, f_d, f_d, f_d, nh_f, d, d, d)
    off = np.concatenate([[0], np.cumsum(sizes)]).tolist()
    dz_col0 = off[6]

    x = jnp.concatenate([x_prompt.reshape(rp, d), x_sample.reshape(nb, d), jnp.zeros((TM - nb, d), F32)], axis=0)
    c_rows = jnp.concatenate([c_prompt, c_sample, jnp.zeros((16 - bp - nb, d), F32)], axis=0)
    mod_all = _ada(c_rows, w_ada, b_ada)
    logf_hm = jnp.transpose(cache_logf, (0, 1, 3, 2))
    k_hm = jnp.transpose(cache_k, (0, 1, 3, 2, 4))
    w_in_t = jnp.transpose(w_in, (2, 0, 1))
    v_hm = jnp.transpose(cache_v, (0, 1, 3, 2, 4))
    zero_state = jnp.zeros((bp, nh_d, dk, dv), F32)

    def with_sample(prompt_rows, sample_rows):
        pad = jnp.zeros((TM - nb, prompt_rows.shape[1]), prompt_rows.dtype)
        return jnp.concatenate([prompt_rows, sample_rows.astype(prompt_rows.dtype), pad], axis=0)

    def to_chunks(rows, n_rows=DELTA_CHUNK):
        return jnp.pad(rows[:, None, :], ((0, 0), (0, n_rows - 1), (0, 0))).reshape(nb * n_rows, -1)

    def head_rows(a):
        return jnp.pad(a, ((0, 0), (0, HEAD_ROWS - nh_f), (0, 0)))

    outs_p, outs_s = [], []
    for l in range(n_layers):
        mod = mod_all[l]
        modx = jnp.concatenate([jnp.broadcast_to(mod[:bp, None, :], (bp, SUBLANES, 6 * d)),
                                mod[None, bp:bp + nb, :]], axis=0)
        sh_m, sc_m, gt_m, sh_f, sc_f, gt_f = jnp.split(modx, 6, axis=-1)

        h = _norm(x, g_mix[l], sc_m, sh_m, tiles_per_seq)
        z_a = _in_proj(h, w_in_t, l, 0, off[7], False, F32)
        z_f = _in_proj(h, w_in_t, l, off[9], 3 * f_d, False, F32)
        gates = _in_proj(h, w_in_t, l, off[13], 3 * d, True, BF16)
        act = _in_small(h, w_in_t, l, off[7], off[12], a_log[l], dt_bias[l], b_forget[l])
        z_a_s = z_a[rp:rp + nb]
        act_s = act[rp:rp + nb]

        ya_p, sconv_p = _conv_prompt(z_a, w_sconv[l], 0, cw, bp, seq, True, BF16)
        ya_s, sconv_s = _conv_step(z_a_s, w_sconv[l], state_sconv[l], 0, cw, True, BF16)
        taps_a = w_sconv.shape[1]

        qkv_p, dconv_p = _conv_prompt(z_a, w_dconv[l], off[3], 2 * qk_d + v_d, bp, seq, False, F32)
        qkv_s, dconv_s = _conv_step(z_a_s, w_dconv[l], state_delta_conv[l], off[3], 2 * qk_d + v_d, False, F32)
        taps_b = w_dconv.shape[1]
        bb, cb = _delta_prep(act, nh_d)
        yb_p, delta_p = _delta(qkv_p, bb, cb, z_a, dz_col0, zero_state, g_dnorm[l], bp, seq)
        act_s_rows = to_chunks(jnp.where(jnp.arange(LANES)[None, :] < 2 * nh_d, act_s, 0.0))
        bb_s, cb_s = _delta_prep(act_s_rows, nh_d)
        yb_s_chunks, delta_s = _delta(to_chunks(qkv_s), bb_s, cb_s, to_chunks(z_a_s[:, dz_col0:dz_col0 + v_d]), 0,
                                      state_delta[l], g_dnorm[l], nb, DELTA_CHUNK)
        yb_s = yb_s_chunks[::DELTA_CHUNK]

        qn, kn, knb, fcum, k_hm_p, v_hm_p = _qkprep(z_f, act, g_qnorm[l], g_knorm[l], nh_f, bp, tiles_per_seq)
        yc_p = _flash(qn, knb, v_hm_p, fcum, nh_f, 2 * nh_d, bp, seq, tq)
        q_s = qn[rp:rp + nb].astype(F32).reshape(nb, nh_f, hd)
        k_s = kn[rp:rp + nb].reshape(nb, nh_f, hd)
        v_s = z_f[rp:rp + nb, 2 * f_d:].reshape(nb, nh_f, hd)
        logf_s = act_s[:, 2 * nh_d:2 * nh_d + nh_f]
        logf_s_b = jnp.broadcast_to(head_rows(logf_s[:, :, None]), (nb, HEAD_ROWS, page))
        yc_s = _decode_attention(head_rows(q_s), head_rows(k_s), head_rows(v_s), logf_s_b,
                                 logf_hm, k_hm, v_hm, l, page_table)[:, :nh_f].reshape(nb, f_d)

        ya = with_sample(ya_p, ya_s)
        yb = with_sample(yb_p, yb_s)
        yc = with_sample(yc_p, yc_s)
        merged = _merge(ya, yb, yc, w_oa, w_ob, w_oc, l, gates)
        x = _residual_mm(merged, w_o, l, x, gt_m, tiles_per_seq)

        h2p, idx, wts = _norm_router(x, g_ffn[l], sc_f, sh_f, w_router[l], b_router[l], tiles_per_seq)
        x = _moe(x, h2p, idx, wts, gt_f, w_gate, b_gate, w_up, b_up, w_down, b_down, l, n_tokens, seq)

        outs_p.append((jnp.transpose(k_hm_p, (0, 2, 1, 3)),
                       jnp.transpose(v_hm_p, (0, 2, 1, 3)),
                       act[:rp, 2 * nh_d:2 * nh_d + nh_f].reshape(bp, seq, nh_f),
                       delta_p,
                       dconv_p[:, SUBLANES - (taps_b - 1):],
                       sconv_p[:, SUBLANES - (taps_a - 1):]))
        outs_s.append((k_s.reshape(nb, 1, nh_f, hd), v_s.reshape(nb, 1, nh_f, hd), logf_s.reshape(nb, 1, nh_f),
                       delta_s, dconv_s, sconv_s))

    def stack(sts, i):
        return jnp.stack([st[i] for st in sts], axis=0)

    return (x[:rp].reshape(bp, seq, d), x[rp:rp + nb].reshape(nb, 1, d),
            stack(outs_p, 0), stack(outs_p, 1), stack(outs_p, 2), stack(outs_p, 3), stack(outs_p, 4), stack(outs_p, 5),
            stack(outs_s, 0), stack(outs_s, 1), stack(outs_s, 2), stack(outs_s, 3), stack(outs_s, 4), stack(outs_s, 5))
```

```python
import functools

import jax
import jax.numpy as jnp
import numpy as np
from jax import lax
from jax.experimental import pallas as pl
from jax.experimental.pallas import tpu as pltpu

F32 = jnp.float32
BF16 = jnp.bfloat16
I32 = jnp.int32
U32 = jnp.uint32
HI = lax.Precision.HIGHEST

EPS = 1e-6
TOP_K = 4
DELTA_CHUNK = 64
SWIGLU_ALPHA = 1.702
SWIGLU_LIMIT = 7.0
LANES = 128
SUBLANES = 8
HEAD_ROWS = 16
TM = 256
NEG = -1e30
VMEM_LIMIT = 56 * 1024 * 1024


def _cparams(n_axes, vmem=None, unchecked=False):
    return pltpu.CompilerParams(dimension_semantics=("arbitrary",) * n_axes,
                                vmem_limit_bytes=vmem or VMEM_LIMIT,
                                disable_bounds_checks=unchecked)


def _dot(a, b, precision=None):
    return jnp.dot(a, b, preferred_element_type=F32, precision=precision)


def _dot_nt(a, b, precision=None):
    return lax.dot_general(a, b, (((1,), (1,)), ((), ())), preferred_element_type=F32, precision=precision)


def _dot_tn(a, b, precision=None):
    return lax.dot_general(a, b, (((0,), (0,)), ((), ())), preferred_element_type=F32, precision=precision)


def _bdot(a, b):
    return lax.dot_general(a, b, (((2,), (1,)), ((0,), (0,))), preferred_element_type=F32)


def _bdot_nt(a, b):
    return lax.dot_general(a, b, (((2,), (2,)), ((0,), (0,))), preferred_element_type=F32)


def _split3(x):
    p1 = x.astype(BF16)
    r1 = x - p1.astype(F32)
    p2 = r1.astype(BF16)
    p3 = (r1 - p2.astype(F32)).astype(BF16)
    return p1, p2, p3


def _split2(x):
    hi = x.astype(BF16)
    return hi, (x - hi.astype(F32)).astype(BF16)


def _select_dot(x, sel, nt=False):
    d = _dot_nt if nt else _dot
    if nt:
        return sum(d(sel, p) for p in _split3(x))
    return sum(d(p, sel) for p in _split3(x))


def _dot3(a_hi, a_lo, b_hi, b_lo):
    return _dot(a_hi, b_hi) + (_dot(a_hi, b_lo) + _dot(a_lo, b_hi))


def _tile(n, preferred):
    t = preferred
    while n % t:
        t //= 2
    assert t >= LANES, (n, preferred)
    return t


def _iota(shape, dim):
    return lax.broadcasted_iota(I32, shape, dim)


def _sigmoid(x):
    return 1.0 / (1.0 + jnp.exp(-x))


def _ada_kernel(c_ref, w_ref, b_ref, o_ref):
    c = c_ref[...]
    a = (c * _sigmoid(c)).astype(BF16)
    o_ref[...] = _dot(a, w_ref[...].astype(BF16)) + b_ref[...]


def _ada(c_rows, w_ada, b_ada):
    n_layers, d, n = w_ada.shape
    tn = _tile(n, 512)
    return pl.pallas_call(
        _ada_kernel,
        grid=(n_layers, n // tn),
        in_specs=[pl.BlockSpec((16, d), lambda l, j: (0, 0)),
                  pl.BlockSpec((None, d, tn), lambda l, j: (l, 0, j)),
                  pl.BlockSpec((None, 1, tn), lambda l, j: (l, 0, j))],
        out_specs=pl.BlockSpec((None, 16, tn), lambda l, j: (l, 0, j)),
        out_shape=jax.ShapeDtypeStruct((n_layers, 16, n), F32),
        compiler_params=_cparams(2),
    )(c_rows, w_ada, b_ada.reshape(n_layers, 1, n))


def _modulated_norm(x, g, sc, sh):
    ms = jnp.mean(x * x, axis=-1, keepdims=True)
    y = x * lax.rsqrt(ms + EPS) * g
    tm, d = y.shape
    y3 = y.reshape(tm // SUBLANES, SUBLANES, d)
    return (y3 * (1.0 + sc[None]) + sh[None]).reshape(tm, d)


def _norm_kernel(x_ref, g_ref, sc_ref, sh_ref, o_ref):
    o_ref[...] = _modulated_norm(x_ref[...], g_ref[...], sc_ref[...], sh_ref[...]).astype(o_ref.dtype)


def _mod_index(tiles_per_seq, n_prompt):
    return lambda i: (jnp.minimum(i // tiles_per_seq, n_prompt), 0, 0)


def _norm(x, g, sc, sh, tiles_per_seq):
    r, d = x.shape
    n_prompt = sc.shape[0] - 1
    mod_spec = pl.BlockSpec((None, SUBLANES, d), _mod_index(tiles_per_seq, n_prompt))
    return pl.pallas_call(
        _norm_kernel,
        grid=(r // TM,),
        in_specs=[pl.BlockSpec((TM, d), lambda i: (i, 0)),
                  pl.BlockSpec((1, d), lambda i: (0, 0)),
                  mod_spec, mod_spec],
        out_specs=pl.BlockSpec((TM, d), lambda i: (i, 0)),
        out_shape=jax.ShapeDtypeStruct((r, d), BF16),
        compiler_params=_cparams(1),
    )(x, g.reshape(1, d), sc, sh)


def _norm_router_kernel(n_experts, x_ref, g_ref, sc_ref, sh_ref, wr_ref, br_ref, hp_ref, idx_ref, wt_ref):
    h = _modulated_norm(x_ref[...], g_ref[...], sc_ref[...], sh_ref[...])
    tm, d = h.shape
    lo = pltpu.bitcast(h[:, :d // 2].astype(BF16).astype(F32), U32)
    hi = pltpu.bitcast(h[:, d // 2:].astype(BF16).astype(F32), U32)
    hp_ref[...] = (hi & jnp.uint32(0xFFFF0000)) | (lo >> 16)
    logits = _dot(h, wr_ref[...], HI) + br_ref[...]
    lane = _iota(logits.shape, 1)
    work = jnp.where(lane < n_experts, logits, NEG)
    vals, idxs = [], []
    for _ in range(TOP_K):
        m = jnp.max(work, axis=-1, keepdims=True)
        ix = jnp.min(jnp.where(work == m, lane, LANES), axis=-1, keepdims=True)
        vals.append(m)
        idxs.append(ix)
        work = jnp.where(lane == ix, NEG, work)
    es = [jnp.exp(v - vals[0]) for v in vals]
    tot = es[0] + es[1] + es[2] + es[3]
    idx_out = jnp.zeros(logits.shape, I32)
    wt_out = jnp.zeros(logits.shape, F32)
    for k in range(TOP_K):
        idx_out = jnp.where(lane == k, idxs[k], idx_out)
        wt_out = jnp.where(lane == k, es[k] / tot, wt_out)
    idx_ref[...] = idx_out
    wt_ref[...] = wt_out


def _norm_router(x, g, sc, sh, w_router, b_router, tiles_per_seq):
    r, d = x.shape
    n_experts = w_router.shape[1]
    n_prompt = sc.shape[0] - 1
    wr = jnp.pad(w_router, ((0, 0), (0, LANES - n_experts)))
    br = jnp.pad(b_router, (0, LANES - n_experts)).reshape(1, LANES)
    mod_spec = pl.BlockSpec((None, SUBLANES, d), _mod_index(tiles_per_seq, n_prompt))
    return pl.pallas_call(
        functools.partial(_norm_router_kernel, n_experts),
        grid=(r // TM,),
        in_specs=[pl.BlockSpec((TM, d), lambda i: (i, 0)),
                  pl.BlockSpec((1, d), lambda i: (0, 0)),
                  mod_spec, mod_spec,
                  pl.BlockSpec((d, LANES), lambda i: (0, 0)),
                  pl.BlockSpec((1, LANES), lambda i: (0, 0))],
        out_specs=[pl.BlockSpec((TM, d // 2), lambda i: (i, 0)),
                   pl.BlockSpec((TM, LANES), lambda i: (i, 0)),
                   pl.BlockSpec((TM, LANES), lambda i: (i, 0))],
        out_shape=[jax.ShapeDtypeStruct((r, d // 2), U32),
                   jax.ShapeDtypeStruct((r, LANES), I32),
                   jax.ShapeDtypeStruct((r, LANES), F32)],
        compiler_params=_cparams(1),
    )(x, g.reshape(1, d), sc, sh, wr, br)


XPOSE = 512


def _transpose_to_bf16(src, dst):
    for kc in range(dst.shape[0] // XPOSE):
        dst[kc * XPOSE:(kc + 1) * XPOSE, :] = src[:, kc * XPOSE:(kc + 1) * XPOSE].T.astype(BF16)


def _in_proj_kernel(layer, row0, squash, h_ref, wt_ref, o_ref, wbuf, ws, sem):
    j = pl.program_id(0)
    i = pl.program_id(1)
    tn = ws.shape[1]

    def copy(tile, slot):
        return pltpu.make_async_copy(wt_ref.at[pl.ds(row0 + tile * tn, tn), layer], wbuf.at[slot], sem.at[slot])

    @pl.when(i == 0)
    def _():
        @pl.when(j == 0)
        def _():
            copy(0, 0).start()

        @pl.when(j + 1 < pl.num_programs(0))
        def _():
            copy(j + 1, (j + 1) % 2).start()

        slot = j % 2
        copy(j, slot).wait()
        _transpose_to_bf16(wbuf.at[slot], ws)

    y = _dot(h_ref[...], ws[...])
    if squash:
        y = _sigmoid(y)
    o_ref[...] = y.astype(o_ref.dtype)


def _in_proj(h, wt, layer, col0, width, squash, out_dtype):
    r, d = h.shape
    wide = [t for t in (1024, 768, 512) if width % t == 0]
    tn = wide[0] if wide else _tile(width, 1024)
    assert d % XPOSE == 0 or d == XPOSE
    return pl.pallas_call(
        functools.partial(_in_proj_kernel, layer, col0, squash),
        grid=(width // tn, r // TM),
        in_specs=[pl.BlockSpec((TM, d), lambda j, i: (i, 0)),
                  pl.BlockSpec(memory_space=pl.ANY)],
        out_specs=pl.BlockSpec((TM, tn), lambda j, i: (i, j)),
        out_shape=jax.ShapeDtypeStruct((r, width), out_dtype),
        scratch_shapes=[pltpu.VMEM((2, tn, d), F32), pltpu.VMEM((d, tn), BF16), pltpu.SemaphoreType.DMA((2,))],
        compiler_params=_cparams(2),
    )(h, wt)


def _in_small_kernel(layer, row_a, row_b, nh_d, nh_f, h_ref, wt_ref, p_ref, o_ref, wbuf, was, wbs, sem):
    @pl.when(pl.program_id(0) == 0)
    def _():
        ca = pltpu.make_async_copy(wt_ref.at[pl.ds(row_a, LANES), layer], wbuf.at[0], sem.at[0])
        cb = pltpu.make_async_copy(wt_ref.at[pl.ds(row_b, LANES), layer], wbuf.at[1], sem.at[1])
        ca.start()
        cb.start()
        ca.wait()
        cb.wait()
        _transpose_to_bf16(wbuf.at[0], was)
        _transpose_to_bf16(wbuf.at[1], wbs)

    h = h_ref[...]
    lane = _iota((h.shape[0], LANES), 1)
    z = jnp.where(lane < 2 * nh_d, _dot(h, was[...]), _dot(h, wbs[...]))
    beta = _sigmoid(z)
    za = z + p_ref[1:2, :]
    softplus = jnp.maximum(za, 0.0) + jnp.log1p(jnp.exp(-jnp.abs(za)))
    g = -jnp.exp(p_ref[0:1, :]) * softplus
    zf = z + p_ref[2:3, :]
    logf = jnp.minimum(zf, 0.0) - jnp.log1p(jnp.exp(-jnp.abs(zf)))
    o_ref[...] = jnp.where(lane < nh_d, beta, jnp.where(lane < 2 * nh_d, g, logf))


def _in_small(h, wt, layer, col_ba, col_f, a_log, dt_bias, b_forget):
    r, d = h.shape
    nh_d, nh_f = a_log.shape[0], b_forget.shape[0]
    row_b = col_f - 2 * nh_d
    assert col_ba + LANES <= wt.shape[0] and row_b >= 0 and row_b + LANES <= wt.shape[0]
    p = jnp.zeros((SUBLANES, LANES), F32)
    p = p.at[0, nh_d:2 * nh_d].set(a_log).at[1, nh_d:2 * nh_d].set(dt_bias)
    p = p.at[2, 2 * nh_d:2 * nh_d + nh_f].set(b_forget)
    return pl.pallas_call(
        functools.partial(_in_small_kernel, layer, col_ba, row_b, nh_d, nh_f),
        grid=(r // TM,),
        in_specs=[pl.BlockSpec((TM, d), lambda i: (i, 0)),
                  pl.BlockSpec(memory_space=pl.ANY),
                  pl.BlockSpec((SUBLANES, LANES), lambda i: (0, 0))],
        out_specs=pl.BlockSpec((TM, LANES), lambda i: (i, 0)),
        out_shape=jax.ShapeDtypeStruct((r, LANES), F32),
        scratch_shapes=[pltpu.VMEM((2, LANES, d), F32), pltpu.VMEM((d, LANES), BF16),
                        pltpu.VMEM((d, LANES), BF16), pltpu.SemaphoreType.DMA((2,))],
        compiler_params=_cparams(1),
    )(h, wt, p)


def _conv_prompt_kernel(taps, gated, *refs):
    if gated:
        ab_ref, ac_ref, ah_ref, w_ref, y_ref, st_ref, ubuf = refs
    else:
        x_ref, w_ref, y_ref, st_ref, ubuf = refs
    s = pl.program_id(2)
    ts = y_ref.shape[0]

    @pl.when(s == 0)
    def _():
        ubuf[0:SUBLANES, :] = jnp.zeros((SUBLANES, ubuf.shape[1]), F32)

    u = ac_ref[...] * ah_ref[...] if gated else x_ref[...]
    ubuf[SUBLANES:SUBLANES + ts, :] = u
    base = SUBLANES - (taps - 1)
    acc = w_ref[0:1, :] * ubuf[base:base + ts, :]
    for i in range(1, taps):
        acc = acc + w_ref[i:i + 1, :] * ubuf[base + i:base + i + ts, :]
    y = ab_ref[...] * acc if gated else acc * _sigmoid(acc)
    y_ref[...] = y.astype(y_ref.dtype)
    tail = ubuf[ts:ts + SUBLANES, :]
    ubuf[0:SUBLANES, :] = tail

    @pl.when(s == pl.num_programs(2) - 1)
    def _():
        st_ref[...] = tail


def _conv_prompt(z, w, col0, width, n_batch, seq, gated, out_dtype):
    taps = w.shape[0]
    tc = _tile(np.gcd(width, col0) if col0 else width, 512)
    ts = TM
    nt = seq // ts
    grid = (width // tc, n_batch, nt)
    row = lambda j, b, s: b * nt + s
    if gated:
        ins = [pl.BlockSpec((ts, tc), lambda j, b, s, o=o: (row(j, b, s), (col0 + o * width) // tc + j))
               for o in range(3)]
        args = (z, z, z, w)
    else:
        ins = [pl.BlockSpec((ts, tc), lambda j, b, s: (row(j, b, s), col0 // tc + j))]
        args = (z, w)
    ins.append(pl.BlockSpec((taps, tc), lambda j, b, s: (0, j)))
    return pl.pallas_call(
        functools.partial(_conv_prompt_kernel, taps, gated),
        grid=grid,
        in_specs=ins,
        out_specs=[pl.BlockSpec((ts, tc), lambda j, b, s: (row(j, b, s), j)),
                   pl.BlockSpec((None, SUBLANES, tc), lambda j, b, s: (b, 0, j))],
        out_shape=[jax.ShapeDtypeStruct((n_batch * seq, width), out_dtype),
                   jax.ShapeDtypeStruct((n_batch, SUBLANES, width), F32)],
        scratch_shapes=[pltpu.VMEM((ts + SUBLANES, tc), F32)],
        compiler_params=_cparams(3),
    )(*args)


def _conv_step_kernel(taps, gated, *refs):
    if gated:
        ab_ref, ac_ref, ah_ref, w_ref, buf_ref, y_ref, nb_ref = refs
        u = ac_ref[...] * ah_ref[...]
    else:
        x_ref, w_ref, buf_ref, y_ref, nb_ref = refs
        u = x_ref[...]
    acc = w_ref[taps - 1:taps, :] * u
    for i in range(taps - 1):
        acc = acc + w_ref[i:i + 1, :] * buf_ref[i]
    y = ab_ref[...] * acc if gated else acc * _sigmoid(acc)
    y_ref[...] = y.astype(y_ref.dtype)
    for i in range(taps - 2):
        nb_ref[i] = buf_ref[i + 1]
    nb_ref[taps - 2] = u


def _conv_step(z_rows, w, buf, col0, width, gated, out_dtype):
    taps = w.shape[0]
    tc = _tile(np.gcd(width, col0) if col0 else width, 512)
    nb = z_rows.shape[0]
    buf_t = jnp.transpose(buf, (1, 0, 2))
    if gated:
        ins = [pl.BlockSpec((nb, tc), lambda j, o=o: (0, (col0 + o * width) // tc + j)) for o in range(3)]
        args = (z_rows, z_rows, z_rows, w, buf_t)
    else:
        ins = [pl.BlockSpec((nb, tc), lambda j: (0, col0 // tc + j))]
        args = (z_rows, w, buf_t)
    ins += [pl.BlockSpec((taps, tc), lambda j: (0, j)),
            pl.BlockSpec((taps - 1, nb, tc), lambda j: (0, 0, j))]
    y, nbuf = pl.pallas_call(
        functools.partial(_conv_step_kernel, taps, gated),
        grid=(width // tc,),
        in_specs=ins,
        out_specs=[pl.BlockSpec((nb, tc), lambda j: (0, j)),
                   pl.BlockSpec((taps - 1, nb, tc), lambda j: (0, 0, j))],
        out_shape=[jax.ShapeDtypeStruct((nb, width), out_dtype),
                   jax.ShapeDtypeStruct((taps - 1, nb, width), F32)],
        compiler_params=_cparams(1),
    )(*args)
    return y, jnp.transpose(nbuf, (1, 0, 2))


def _delta_prep_kernel(nh, ch, act_ref, bb_ref, cb_ref):
    act = act_ref[...]
    tm = act.shape[0]
    ii = _iota((tm, tm), 0)
    jj = _iota((tm, tm), 1)
    tri = jnp.logical_and(ii >= jj, ii // ch == jj // ch).astype(BF16)
    cum = sum(_dot(tri, p) for p in _split3(act))
    for h in range(nh):
        bb_ref[h] = jnp.broadcast_to(act[:, h:h + 1], (tm, LANES))
        cb_ref[h] = jnp.broadcast_to(cum[:, nh + h:nh + h + 1], (tm, LANES))


def _delta_prep(act, nh):
    r = act.shape[0]
    return pl.pallas_call(
        functools.partial(_delta_prep_kernel, nh, DELTA_CHUNK),
        grid=(r // TM,),
        in_specs=[pl.BlockSpec((TM, LANES), lambda i: (i, 0))],
        out_specs=[pl.BlockSpec((nh, TM, LANES), lambda i: (0, i, 0)),
                   pl.BlockSpec((nh, TM, LANES), lambda i: (0, i, 0))],
        out_shape=[jax.ShapeDtypeStruct((nh, r, LANES), F32),
                   jax.ShapeDtypeStruct((nh, r, LANES), F32)],
        compiler_params=_cparams(1),
    )(act)


def _delta_kernel(hb, q_ref, k_ref, v_ref, bb_ref, cb_ref, dz_ref, s0_ref, gn_ref, o_ref, sf_ref, state):
    c = pl.program_id(2)
    ch = q_ref.shape[0]
    dk = LANES

    @pl.when(c == 0)
    def _():
        state[...] = s0_ref[...]

    def heads(ref):
        return jnp.stack([ref[:, i * dk:(i + 1) * dk] for i in range(hb)], axis=0)

    ii = _iota((ch, LANES), 0)[None]
    jj = _iota((ch, LANES), 1)[None]
    left = jj < ch
    pick0 = jnp.broadcast_to((jj == 0).astype(BF16), (hb, ch, LANES))
    eye_right = (jj - ch == ii).astype(F32)
    zrows = jnp.zeros((hb, ch, LANES), F32)
    zrows_b = jnp.zeros((hb, ch, LANES), BF16)
    n_levels = int(np.log2(ch))

    q = heads(q_ref)
    k = heads(k_ref)
    q = q * lax.rsqrt(jnp.sum(q * q, axis=-1, keepdims=True) + EPS) * (dk ** -0.5)
    k = k * lax.rsqrt(jnp.sum(k * k, axis=-1, keepdims=True) + EPS)
    beta_b = bb_ref[...]
    cum_b = cb_ref[...]
    cum_r = sum(_bdot_nt(pick0, p) for p in _split3(jnp.concatenate([cum_b, zrows], axis=1)))
    diff = cum_b - cum_r
    decay_strict = jnp.exp(jnp.where(jnp.logical_and(left, ii > jj), diff, NEG))
    decay_incl = jnp.exp(jnp.where(jnp.logical_and(left, ii >= jj), diff, NEG))
    kb = k * beta_b
    k_pad = jnp.concatenate([k, zrows], axis=1).astype(BF16)
    la = _bdot_nt(jnp.concatenate([kb, q], axis=1).astype(BF16), k_pad)
    lmat = la[:, :ch] * decay_strict
    attn = la[:, ch:] * decay_incl
    w = jnp.where(left, -lmat, eye_right)
    for _ in range(n_levels):
        w_hi, w_lo = _split2(w)
        p_hi = jnp.where(left, w_hi, jnp.zeros_like(w_hi))
        p_lo = jnp.where(left, w_lo, jnp.zeros_like(w_lo))
        r_hi = jnp.concatenate([w_hi, zrows_b], axis=1)
        r_lo = jnp.concatenate([w_lo, zrows_b], axis=1)
        r = _bdot(p_hi, r_hi) + (_bdot(p_hi, r_lo) + _bdot(p_lo, r_hi))
        w = jnp.where(left, r, w + r)
    t_right = jnp.where(left, 0.0, w).astype(BF16)
    e_cum = jnp.exp(cum_b)
    rhs = jnp.concatenate([heads(v_ref) * beta_b, kb * e_cum], axis=2)
    uw = _bdot(t_right, jnp.concatenate([jnp.zeros_like(rhs), rhs], axis=1).astype(BF16))
    u = uw[:, :, :LANES]
    wk = uw[:, :, LANES:]
    s_prev = state[...]
    sq = _bdot(jnp.concatenate([wk, q * e_cum], axis=1).astype(BF16), s_prev.astype(BF16))
    v_new = u - sq[:, :ch]
    o = sq[:, ch:] + _bdot(attn.astype(BF16), jnp.concatenate([v_new, zrows], axis=1).astype(BF16))
    cum_last = cum_b[:, ch - 1:ch, :]
    k_dec = (k * jnp.exp(cum_last - cum_b)).astype(BF16)
    v_new_b = v_new.astype(BF16)
    s_decayed = s_prev * jnp.exp(cum_last)
    on = o * lax.rsqrt(jnp.mean(o * o, axis=-1, keepdims=True) + EPS) * gn_ref[...][None]
    for i in range(hb):
        state[i] = s_decayed[i] + _dot_tn(k_dec[i], v_new_b[i])
        dz = dz_ref[:, i * dk:(i + 1) * dk]
        o_ref[:, i * dk:(i + 1) * dk] = (on[i] * (dz * _sigmoid(dz))).astype(o_ref.dtype)

    @pl.when(c == pl.num_programs(2) - 1)
    def _():
        sf_ref[...] = state[...]


def _delta(qkv, bb, cb, dz_arr, dz_col0, s0, g_dnorm, n_batch, seq):
    nh, dk, dv = s0.shape[1], s0.shape[2], s0.shape[3]
    ch = DELTA_CHUNK
    nc = seq // ch
    hb = max(c for c in (nh, 4, 2, 1) if nh % c == 0 and dz_col0 % (c * dk) == 0)
    wb = hb * dk
    ng = nh // hb
    row = lambda b, g, c: b * nc + c
    return pl.pallas_call(
        functools.partial(_delta_kernel, hb),
        grid=(n_batch, ng, nc),
        in_specs=[pl.BlockSpec((ch, wb), lambda b, g, c: (row(b, g, c), g)),
                  pl.BlockSpec((ch, wb), lambda b, g, c: (row(b, g, c), ng + g)),
                  pl.BlockSpec((ch, wb), lambda b, g, c: (row(b, g, c), 2 * ng + g)),
                  pl.BlockSpec((hb, ch, LANES), lambda b, g, c: (g, row(b, g, c), 0)),
                  pl.BlockSpec((hb, ch, LANES), lambda b, g, c: (g, row(b, g, c), 0)),
                  pl.BlockSpec((ch, wb), lambda b, g, c: (row(b, g, c), dz_col0 // wb + g)),
                  pl.BlockSpec((None, hb, dk, dv), lambda b, g, c: (b, g, 0, 0)),
                  pl.BlockSpec((1, dv), lambda b, g, c: (0, 0))],
        out_specs=[pl.BlockSpec((ch, wb), lambda b, g, c: (row(b, g, c), g)),
                   pl.BlockSpec((None, hb, dk, dv), lambda b, g, c: (b, g, 0, 0))],
        out_shape=[jax.ShapeDtypeStruct((n_batch * seq, nh * dv), BF16),
                   jax.ShapeDtypeStruct(s0.shape, F32)],
        scratch_shapes=[pltpu.VMEM((hb, dk, dv), F32)],
        compiler_params=_cparams(3),
    )(qkv, qkv, qkv, bb, cb, dz_arr, s0, g_dnorm.reshape(1, dv))


def _qkprep_kernel(nh, hd, q_ref, k_ref, v_ref, act_ref, gq_ref, gk_ref,
                   qn_ref, kn_ref, knb_ref, fc_ref, kh_ref, vh_ref, carry):
    t = pl.program_id(0)
    scale = hd ** -0.5
    for h in range(nh):
        sl = slice(h * hd, (h + 1) * hd)
        q = q_ref[:, sl]
        qn = q * lax.rsqrt(jnp.mean(q * q, axis=-1, keepdims=True) + EPS) * gq_ref[...]
        qn_ref[:, sl] = (qn * scale).astype(BF16)
        k = k_ref[:, sl]
        kn = k * lax.rsqrt(jnp.mean(k * k, axis=-1, keepdims=True) + EPS) * gk_ref[...]
        kn_ref[:, sl] = kn
        knb_ref[:, sl] = kn.astype(BF16)
        kh_ref[h] = kn
        vh_ref[h] = v_ref[:, sl]

    tm = act_ref.shape[0]
    tri = (_iota((tm, tm), 0) >= _iota((tm, tm), 1)).astype(BF16)
    fc = sum(_dot(tri, p) for p in _split3(act_ref[...])) + carry[0:1, :]
    fc_ref[...] = fc
    carry[...] = jnp.broadcast_to(fc[tm - 1:tm, :], carry.shape)


def _qkprep(z_f, act, g_q, g_k, nh, n_prompt, tiles_per_seq):
    r = z_f.shape[0]
    hd = g_q.shape[0]
    w = nh * hd
    n_tiles = r // TM
    n_ptiles = n_prompt * tiles_per_seq
    assert n_tiles == n_ptiles + 1
    rt = lambda t: jnp.where(t == 0, n_ptiles, t - 1)
    pt = lambda t: jnp.maximum(t - 1, 0)
    rows = lambda c: pl.BlockSpec((TM, w), lambda t: (rt(t), c))
    head_major = pl.BlockSpec((None, nh, TM, hd), lambda t: (pt(t) // tiles_per_seq, 0, pt(t) % tiles_per_seq, 0))

    def kern(*refs):
        carry = refs[-1]

        t = pl.program_id(0)

        @pl.when(jnp.logical_or(t == 0, (t - 1) % tiles_per_seq == 0))
        def _():
            carry[...] = jnp.zeros(carry.shape, F32)
        _qkprep_kernel(nh, hd, *refs)

    return pl.pallas_call(
        kern,
        grid=(n_tiles,),
        in_specs=[rows(0), rows(1), rows(2),
                  pl.BlockSpec((TM, LANES), lambda t: (rt(t), 0)),
                  pl.BlockSpec((1, hd), lambda t: (0, 0)),
                  pl.BlockSpec((1, hd), lambda t: (0, 0))],
        out_specs=[rows(0), rows(0), rows(0),
                   pl.BlockSpec((TM, LANES), lambda t: (rt(t), 0)),
                   head_major, head_major],
        out_shape=[jax.ShapeDtypeStruct((r, w), BF16),
                   jax.ShapeDtypeStruct((r, w), F32),
                   jax.ShapeDtypeStruct((r, w), BF16),
                   jax.ShapeDtypeStruct((r, LANES), F32),
                   jax.ShapeDtypeStruct((n_prompt, nh, tiles_per_seq * TM, hd), F32),
                   jax.ShapeDtypeStruct((n_prompt, nh, tiles_per_seq * TM, hd), F32)],
        scratch_shapes=[pltpu.VMEM((SUBLANES, LANES), F32)],
        compiler_params=_cparams(1),
    )(z_f, z_f, z_f, act, g_q.reshape(1, hd), g_k.reshape(1, hd))


def _flash_kernel(lane0, hb, qi_ref, ki_ref, q_ref, k_ref, v_ref, fq_ref, fk_ref, o_ref,
                  m_ref, l_ref, acc_ref, fqb_ref):
    head0 = lane0 + pl.program_id(1) * hb
    qi = qi_ref[pl.program_id(2)]
    ki = ki_ref[pl.program_id(2)]
    tq = q_ref.shape[0]
    tk = k_ref.shape[0]
    hd = q_ref.shape[1] // hb

    def heads(ref):
        return jnp.stack([ref[:, i * hd:(i + 1) * hd] for i in range(hb)], axis=0)

    @pl.when(ki == 0)
    def _():
        m_ref[...] = jnp.full(m_ref.shape, NEG, F32)
        l_ref[...] = jnp.zeros(l_ref.shape, F32)
        acc_ref[...] = jnp.zeros(acc_ref.shape, F32)
        parts = _split3(fq_ref[...])
        for i in range(hb):
            sel = (_iota((LANES, LANES), 0) == head0 + i).astype(BF16)
            fqb_ref[i] = sum(_dot(p, sel) for p in parts)

    def update(on_diagonal):
        s = _bdot_nt(heads(q_ref), heads(k_ref))
        pick = (_iota((SUBLANES, LANES), 1) == head0 + _iota((SUBLANES, LANES), 0)).astype(BF16)
        fk_rows = _select_dot(fk_ref[...], pick, nt=True)
        fk = jnp.stack([fk_rows[i:i + 1, :] for i in range(hb)], axis=0)
        s = s + jnp.concatenate([fqb_ref[...]] * (tk // LANES), axis=2) - fk
        if on_diagonal:
            s = jnp.where((_iota((tq, tk), 1) <= _iota((tq, tk), 0))[None], s, NEG)
        m_prev = m_ref[...]
        m_new = jnp.maximum(m_prev, jnp.max(s, axis=-1, keepdims=True))
        alpha = jnp.exp(m_prev - m_new)
        p = jnp.exp(s - m_new)
        l_ref[...] = alpha * l_ref[...] + jnp.sum(p, axis=-1, keepdims=True)
        acc_ref[...] = alpha * acc_ref[...] + _bdot(p.astype(BF16), v_ref[...].astype(BF16))
        m_ref[...] = m_new

    @pl.when(ki < qi)
    def _():
        update(False)

    @pl.when(ki == qi)
    def _():
        update(True)
        out = acc_ref[...] / l_ref[...]
        for i in range(hb):
            o_ref[:, i * hd:(i + 1) * hd] = out[i].astype(o_ref.dtype)


def _flash(qn, knb, v_hm, fcum, nh, lane0, n_batch, seq, tq):
    hd = qn.shape[1] // nh
    nq = seq // tq
    pairs = [(qi, ki) for qi in range(nq) for ki in range(qi + 1)]
    qi_arr = jnp.asarray([p[0] for p in pairs], I32)
    ki_arr = jnp.asarray([p[1] for p in pairs], I32)
    hb = max(c for c in (4, 2, 1) if nh % c == 0)
    assert hb <= SUBLANES
    qrow = lambda b, g, p, qa, ka: b * nq + qa[p]
    krow = lambda b, g, p, qa, ka: b * nq + ka[p]
    grid_spec = pltpu.PrefetchScalarGridSpec(
        num_scalar_prefetch=2,
        grid=(n_batch, nh // hb, len(pairs)),
        in_specs=[pl.BlockSpec((tq, hb * hd), lambda b, g, p, qa, ka: (qrow(b, g, p, qa, ka), g)),
                  pl.BlockSpec((tq, hb * hd), lambda b, g, p, qa, ka: (krow(b, g, p, qa, ka), g)),
                  pl.BlockSpec((None, hb, tq, hd), lambda b, g, p, qa, ka: (b, g, ka[p], 0)),
                  pl.BlockSpec((tq, LANES), lambda b, g, p, qa, ka: (qrow(b, g, p, qa, ka), 0)),
                  pl.BlockSpec((tq, LANES), lambda b, g, p, qa, ka: (krow(b, g, p, qa, ka), 0))],
        out_specs=pl.BlockSpec((tq, hb * hd), lambda b, g, p, qa, ka: (qrow(b, g, p, qa, ka), g)),
        scratch_shapes=[pltpu.VMEM((hb, tq, 1), F32), pltpu.VMEM((hb, tq, 1), F32),
                        pltpu.VMEM((hb, tq, hd), F32), pltpu.VMEM((hb, tq, LANES), F32)],
    )
    return pl.pallas_call(
        functools.partial(_flash_kernel, lane0, hb),
        grid_spec=grid_spec,
        out_shape=jax.ShapeDtypeStruct((n_batch * seq, nh * hd), BF16),
        compiler_params=_cparams(3),
    )(qi_arr, ki_arr, qn, knb, v_hm, fcum, fcum)


def _decode_kernel(nh, pt_ref, q_ref, kn_ref, vn_ref, lfn_ref, lfa_ref, lfb_ref, ka_ref, kb_ref, va_ref, vb_ref,
                   o_ref, m_ref, l_ref, acc_ref, carry, xpad):
    j = pl.program_id(1)
    page = carry.shape[1]
    hrow = _iota((HEAD_ROWS, LANES), 0)

    @pl.when(j == 0)
    def _():
        m_ref[...] = jnp.full(m_ref.shape, NEG, F32)
        l_ref[...] = jnp.zeros(l_ref.shape, F32)
        acc_ref[...] = jnp.zeros(acc_ref.shape, F32)
        carry[...] = jnp.zeros(carry.shape, F32)
        xpad[...] = jnp.zeros(xpad.shape, F32)

    after = (_iota((page, page), 0) > _iota((page, page), 1)).astype(BF16)
    ones = jnp.ones((page, page), BF16)
    qb = q_ref[...].astype(BF16)

    def page_update(lf_ref, k_ref, v_ref):
        xpad[0:nh, :] = lf_ref[...]
        parts = _split3(xpad[...])
        s = sum(_dot(p, after) for p in parts) + carry[...] + lfn_ref[...]
        carry[...] = carry[...] + sum(_dot(p, ones) for p in parts)
        s_all = _dot_nt(qb, k_ref[...].reshape(nh * page, qb.shape[1]).astype(BF16))
        for h in range(nh):
            s = s + jnp.where(hrow == h, s_all[:, h * page:(h + 1) * page], 0.0)
        m_prev = m_ref[...]
        m_new = jnp.maximum(m_prev, jnp.max(s, axis=-1, keepdims=True))
        alpha = jnp.exp(m_prev - m_new)
        p = jnp.exp(s - m_new)
        l_ref[...] = alpha * l_ref[...] + jnp.sum(p, axis=-1, keepdims=True)
        p_blocks = jnp.concatenate([jnp.where(hrow == h, p, 0.0) for h in range(nh)], axis=1).astype(BF16)
        pv = _dot(p_blocks, v_ref[...].reshape(nh * page, qb.shape[1]).astype(BF16))
        acc_ref[...] = alpha * acc_ref[...] + pv
        m_ref[...] = m_new

    page_update(lfb_ref, kb_ref, vb_ref)
    page_update(lfa_ref, ka_ref, va_ref)

    @pl.when(j == pl.num_programs(1) - 1)
    def _():
        s_new = jnp.sum(q_ref[...] * kn_ref[...], axis=-1, keepdims=True)
        m_prev = m_ref[...]
        m_new = jnp.maximum(m_prev, s_new)
        alpha = jnp.exp(m_prev - m_new)
        pn = jnp.exp(s_new - m_new)
        l_fin = alpha * l_ref[...] + pn
        o_ref[...] = (alpha * acc_ref[...] + pn * vn_ref[...]) / l_fin


def _decode_attention(q, k_new, v_new, logf_new_b, logf_hm, k_hm, v_hm, layer, page_table):
    nb, _, hd = q.shape
    nh = k_hm.shape[2]
    npg = page_table.shape[1]
    page = k_hm.shape[3]
    assert npg % 2 == 0 and page == LANES and hd == LANES
    half = npg // 2
    small = pl.BlockSpec((None, HEAD_ROWS, hd), lambda b, j, pt: (b, 0, 0))
    pa = lambda b, j, pt: pt[b, npg - 2 - 2 * j]
    pb = lambda b, j, pt: pt[b, npg - 1 - 2 * j]
    lf = lambda pg: pl.BlockSpec((None, None, nh, page), lambda b, j, pt: (layer, pg(b, j, pt), 0, 0))
    cache = lambda pg: pl.BlockSpec((None, None, nh, page, hd), lambda b, j, pt: (layer, pg(b, j, pt), 0, 0, 0))
    grid_spec = pltpu.PrefetchScalarGridSpec(
        num_scalar_prefetch=1,
        grid=(nb, half),
        in_specs=[small, small, small,
                  pl.BlockSpec((None, HEAD_ROWS, page), lambda b, j, pt: (b, 0, 0)),
                  lf(pa), lf(pb), cache(pa), cache(pb), cache(pa), cache(pb)],
        out_specs=pl.BlockSpec((None, HEAD_ROWS, hd), lambda b, j, pt: (b, 0, 0)),
        scratch_shapes=[pltpu.VMEM((HEAD_ROWS, 1), F32), pltpu.VMEM((HEAD_ROWS, 1), F32),
                        pltpu.VMEM((HEAD_ROWS, hd), F32), pltpu.VMEM((HEAD_ROWS, page), F32),
                        pltpu.VMEM((HEAD_ROWS, page), F32)],
    )
    return pl.pallas_call(
        functools.partial(_decode_kernel, nh),
        grid_spec=grid_spec,
        out_shape=jax.ShapeDtypeStruct((nb, HEAD_ROWS, hd), F32),
        compiler_params=_cparams(2),
    )(page_table, q, k_new, v_new, logf_new_b, logf_hm, logf_hm, k_hm, k_hm, v_hm, v_hm)


def _merge_kernel(ya_ref, yb_ref, yc_ref, wa_ref, wb_ref, wc_ref, ga_ref, gb_ref, gc_ref, o_ref,
                  was, wbs, wcs):
    @pl.when(pl.program_id(1) == 0)
    def _():
        was[...] = wa_ref[...].astype(BF16)
        wbs[...] = wb_ref[...].astype(BF16)
        wcs[...] = wc_ref[...].astype(BF16)

    acc = ga_ref[...].astype(F32) * _dot(ya_ref[...], was[...])
    acc = acc + gb_ref[...].astype(F32) * _dot(yb_ref[...], wbs[...])
    acc = acc + gc_ref[...].astype(F32) * _dot(yc_ref[...], wcs[...])
    o_ref[...] = acc.astype(o_ref.dtype)


def _merge(ya, yb, yc, w_oa, w_ob, w_oc, layer, gates):
    r = ya.shape[0]
    ka, kb, kc = ya.shape[1], yb.shape[1], yc.shape[1]
    d = w_oa.shape[2]
    tn = _tile(d, 1024)
    nj = d // tn
    wspec = lambda k: pl.BlockSpec((None, k, tn), lambda j, i: (layer, 0, j))
    gspec = lambda o: pl.BlockSpec((TM, tn), lambda j, i: (i, o * nj + j))
    return pl.pallas_call(
        _merge_kernel,
        grid=(nj, r // TM),
        in_specs=[pl.BlockSpec((TM, ka), lambda j, i: (i, 0)),
                  pl.BlockSpec((TM, kb), lambda j, i: (i, 0)),
                  pl.BlockSpec((TM, kc), lambda j, i: (i, 0)),
                  wspec(ka), wspec(kb), wspec(kc), gspec(0), gspec(1), gspec(2)],
        out_specs=pl.BlockSpec((TM, tn), lambda j, i: (i, j)),
        out_shape=jax.ShapeDtypeStruct((r, d), BF16),
        scratch_shapes=[pltpu.VMEM((ka, tn), BF16), pltpu.VMEM((kb, tn), BF16), pltpu.VMEM((kc, tn), BF16)],
        compiler_params=_cparams(2),
    )(ya, yb, yc, w_oa, w_ob, w_oc, gates, gates, gates)


def _residual_mm_kernel(m_ref, w_ref, x_ref, gt_ref, o_ref, ws):
    @pl.when(pl.program_id(1) == 0)
    def _():
        ws[...] = w_ref[...].astype(BF16)

    y = _dot(m_ref[...], ws[...])
    tm, tn = y.shape
    y3 = y.reshape(tm // SUBLANES, SUBLANES, tn) * gt_ref[...][None]
    o_ref[...] = x_ref[...] + y3.reshape(tm, tn)


def _residual_mm(m, w_o, layer, x, gt, tiles_per_seq):
    r, k = m.shape
    d = w_o.shape[2]
    tn = _tile(d, 1024)
    n_prompt = gt.shape[0] - 1
    return pl.pallas_call(
        _residual_mm_kernel,
        grid=(d // tn, r // TM),
        in_specs=[pl.BlockSpec((TM, k), lambda j, i: (i, 0)),
                  pl.BlockSpec((None, k, tn), lambda j, i: (layer, 0, j)),
                  pl.BlockSpec((TM, tn), lambda j, i: (i, j)),
                  pl.BlockSpec((None, SUBLANES, tn),
                               lambda j, i: (jnp.minimum(i // tiles_per_seq, n_prompt), 0, j))],
        out_specs=pl.BlockSpec((TM, tn), lambda j, i: (i, j)),
        out_shape=jax.ShapeDtypeStruct((r, d), F32),
        scratch_shapes=[pltpu.VMEM((k, tn), BF16)],
        compiler_params=_cparams(2),
    )(m, w_o, x, gt)


ISSUE_UNROLL = 8


def _gather_rows_kernel(tok_ref, na_ref, src_ref, o_ref, buf, sem):
    t = pl.program_id(0)
    n_act = na_ref[0]
    tm = o_ref.shape[0]

    def copy(tile, r, slot):
        return pltpu.make_async_copy(src_ref.at[pl.ds(tok_ref[tile * tm + r], 1)],
                                     buf.at[slot, pl.ds(r, 1)], sem.at[slot])

    def issue(tile, slot):
        def body(g, carry):
            for u in range(ISSUE_UNROLL):
                copy(tile, g * ISSUE_UNROLL + u, slot).start(priority=u % 2)
            return carry
        lax.fori_loop(0, tm // ISSUE_UNROLL, body, 0)

    @pl.when(jnp.logical_and(t == 0, n_act > 0))
    def _():
        issue(0, 0)

    @pl.when(t + 1 < n_act)
    def _():
        issue(t + 1, (t + 1) % 2)

    @pl.when(t < n_act)
    def _():
        slot = t % 2
        pltpu.make_async_copy(buf.at[slot], buf.at[slot], sem.at[slot]).wait()
        o_ref[...] = buf[slot]

    @pl.when(t >= n_act)
    def _():
        o_ref[...] = jnp.zeros(o_ref.shape, o_ref.dtype)


def _gather_rows(src, tok, n_active, n_out):
    w = src.shape[1]
    grid_spec = pltpu.PrefetchScalarGridSpec(
        num_scalar_prefetch=2,
        grid=(n_out // TM,),
        in_specs=[pl.BlockSpec(memory_space=pl.ANY)],
        out_specs=pl.BlockSpec((TM, w), lambda t, tok, na: (t, 0)),
        scratch_shapes=[pltpu.VMEM((2, TM, w), src.dtype), pltpu.SemaphoreType.DMA((2,))],
    )
    return pl.pallas_call(
        _gather_rows_kernel,
        grid_spec=grid_spec,
        out_shape=jax.ShapeDtypeStruct((n_out, w), src.dtype),
        compiler_params=_cparams(1, unchecked=True),
    )(tok, n_active, src)


def _unpack(xp):
    lo = pltpu.bitcast(xp << 16, F32).astype(BF16)
    hi = pltpu.bitcast(xp & jnp.uint32(0xFFFF0000), F32).astype(BF16)
    return lo, hi


def _expert_up_kernel(te_ref, na_ref, x_ref, wg_ref, wu_ref, bg_ref, bu_ref, o_ref, wgs, wus):
    t = pl.program_id(1)
    fresh = jnp.logical_or(t == 0, te_ref[t] != te_ref[jnp.maximum(t - 1, 0)])

    @pl.when(fresh)
    def _():
        wgs[...] = wg_ref[...].astype(BF16)
        wus[...] = wu_ref[...].astype(BF16)

    @pl.when(t < na_ref[0])
    def _():
        lo, hi = _unpack(x_ref[...])
        half = lo.shape[1]
        g = _dot(lo, wgs[0:half, :]) + _dot(hi, wgs[half:, :]) + bg_ref[...]
        u = _dot(lo, wus[0:half, :]) + _dot(hi, wus[half:, :]) + bu_ref[...]
        g = jnp.minimum(g, SWIGLU_LIMIT)
        u = jnp.clip(u, -SWIGLU_LIMIT, SWIGLU_LIMIT)
        o_ref[...] = (g * _sigmoid(SWIGLU_ALPHA * g) * (u + 1.0)).astype(o_ref.dtype)

    @pl.when(t >= na_ref[0])
    def _():
        o_ref[...] = jnp.zeros(o_ref.shape, o_ref.dtype)


def _expert_up(xs, w_gate, w_up, b_gate, b_up, layer, tile_expert, n_active):
    p, half = xs.shape
    n_exp, d, f = w_gate.shape[1], w_gate.shape[2], w_gate.shape[3]
    tf = _tile(f, 512)
    n_tiles = p // TM
    xrow = lambda fc, t, te, na: (jnp.minimum(t, na[0] - 1), 0)
    wspec = pl.BlockSpec((None, None, d, tf), lambda fc, t, te, na: (layer, te[t], 0, fc))
    bspec = pl.BlockSpec((None, None, 1, tf), lambda fc, t, te, na: (layer, te[t], 0, fc))
    grid_spec = pltpu.PrefetchScalarGridSpec(
        num_scalar_prefetch=2,
        grid=(f // tf, n_tiles),
        in_specs=[pl.BlockSpec((TM, half), xrow), wspec, wspec, bspec, bspec],
        out_specs=pl.BlockSpec((TM, tf), lambda fc, t, te, na: (t, fc)),
        scratch_shapes=[pltpu.VMEM((d, tf), BF16), pltpu.VMEM((d, tf), BF16)],
    )
    n_layers = w_gate.shape[0]
    return pl.pallas_call(
        _expert_up_kernel,
        grid_spec=grid_spec,
        out_shape=jax.ShapeDtypeStruct((p, f), BF16),
        compiler_params=_cparams(2),
    )(tile_expert, n_active, xs, w_gate, w_up,
      b_gate.reshape(n_layers, n_exp, 1, f), b_up.reshape(n_layers, n_exp, 1, f))


def _expert_down_kernel(te_ref, na_ref, h_ref, w_ref, b_ref, o_ref, ws):
    t = pl.program_id(0)
    fresh = jnp.logical_or(t == 0, te_ref[t] != te_ref[jnp.maximum(t - 1, 0)])

    @pl.when(fresh)
    def _():
        ws[...] = w_ref[...].astype(BF16)

    o_ref[...] = _dot(h_ref[...], ws[...]) + b_ref[...]


def _expert_down(hmid, w_down, b_down, layer, tile_expert, n_active):
    p, f = hmid.shape
    n_layers, n_exp, _, d = w_down.shape
    grid_spec = pltpu.PrefetchScalarGridSpec(
        num_scalar_prefetch=2,
        grid=(p // TM,),
        in_specs=[pl.BlockSpec((TM, f), lambda t, te, na: (t, 0)),
                  pl.BlockSpec((None, None, f, d), lambda t, te, na: (layer, te[t], 0, 0)),
                  pl.BlockSpec((None, None, 1, d), lambda t, te, na: (layer, te[t], 0, 0))],
        out_specs=pl.BlockSpec((TM, d), lambda t, te, na: (t, 0)),
        scratch_shapes=[pltpu.VMEM((f, d), BF16)],
    )
    return pl.pallas_call(
        _expert_down_kernel,
        grid_spec=grid_spec,
        out_shape=jax.ShapeDtypeStruct((p, d), F32),
        compiler_params=_cparams(1),
    )(tile_expert, n_active, hmid, w_down, b_down.reshape(n_layers, n_exp, 1, d))


COMBINE_ROWS = 64


def _combine_kernel(pos_ref, y_ref, x_ref, wt_ref, gt_ref, o_ref, buf, sem):
    i = pl.program_id(0)
    tm = x_ref.shape[0]

    def issue(tile, slot):
        def body(r, carry):
            for k in range(TOP_K):
                src = pos_ref[(tile * tm + r) * TOP_K + k]
                pltpu.make_async_copy(y_ref.at[pl.ds(src, 1)], buf.at[slot, k, pl.ds(r, 1)],
                                      sem.at[slot]).start(priority=k % 2)
            return carry
        lax.fori_loop(0, tm, body, 0, unroll=ISSUE_UNROLL // TOP_K)

    @pl.when(i == 0)
    def _():
        issue(0, 0)

    @pl.when(i + 1 < pl.num_programs(0))
    def _():
        issue(i + 1, (i + 1) % 2)

    slot = i % 2
    pltpu.make_async_copy(buf.at[slot], buf.at[slot], sem.at[slot]).wait()
    wt = wt_ref[...]
    acc = wt[:, 0:1] * buf[slot, 0]
    for k in range(1, TOP_K):
        acc = acc + wt[:, k:k + 1] * buf[slot, k]
    d = acc.shape[1]
    a3 = acc.reshape(tm // SUBLANES, SUBLANES, d) * gt_ref[...][None]
    o_ref[...] = x_ref[...] + a3.reshape(tm, d)


def _combine(y, pos, x, wts, gt, rows_per_seq):
    r, d = x.shape
    tm = COMBINE_ROWS
    n_prompt = gt.shape[0] - 1
    tiles_per_seq = rows_per_seq // tm
    grid_spec = pltpu.PrefetchScalarGridSpec(
        num_scalar_prefetch=1,
        grid=(r // tm,),
        in_specs=[pl.BlockSpec(memory_space=pl.ANY),
                  pl.BlockSpec((tm, d), lambda i, pos: (i, 0)),
                  pl.BlockSpec((tm, LANES), lambda i, pos: (i, 0)),
                  pl.BlockSpec((None, SUBLANES, d),
                               lambda i, pos: (jnp.minimum(i // tiles_per_seq, n_prompt), 0, 0))],
        out_specs=pl.BlockSpec((tm, d), lambda i, pos: (i, 0)),
        scratch_shapes=[pltpu.VMEM((2, TOP_K, tm, d), F32), pltpu.SemaphoreType.DMA((2,))],
    )
    return pl.pallas_call(
        _combine_kernel,
        grid_spec=grid_spec,
        out_shape=jax.ShapeDtypeStruct((r, d), F32),
        compiler_params=_cparams(1, unchecked=True),
    )(pos, y, x, wts, gt)


def _dispatch_plan(idx, n_tokens, n_experts, n_rows_max):
    eid = idx[:n_tokens, :TOP_K].reshape(-1)
    n_flat = eid.shape[0]
    order = jnp.argsort(eid, stable=True).astype(I32)
    rank = jnp.argsort(order).astype(I32)
    counts = jnp.sum((eid[:, None] == jnp.arange(n_experts, dtype=I32)[None, :]).astype(I32), axis=0)
    padded = ((counts + TM - 1) // TM) * TM
    ends_p = jnp.cumsum(padded)
    starts_p = ends_p - padded
    starts = jnp.cumsum(counts) - counts
    pos_flat = starts_p[eid] + (rank - starts[eid])
    n_active = (ends_p[-1] // TM).astype(I32)
    tile_start = jnp.arange(n_rows_max // TM, dtype=I32) * TM
    tile_expert = jnp.sum((tile_start[:, None] >= ends_p[None, :]).astype(I32), axis=1)
    tile_expert = jnp.minimum(tile_expert, n_experts - 1)
    last_e = jnp.max(jnp.where(counts > 0, jnp.arange(n_experts, dtype=I32), 0))
    tile_expert = jnp.where(tile_start < ends_p[-1], tile_expert, last_e).astype(I32)
    rows = jnp.arange(n_rows_max, dtype=I32)
    e_row = jnp.repeat(tile_expert, TM)
    off_row = rows - starts_p[e_row]
    src = jnp.clip(starts[e_row] + off_row, 0, n_flat - 1)
    row_token = jnp.where(off_row < counts[e_row], order[src] // TOP_K, 0).astype(I32)
    return row_token, pos_flat.astype(I32), tile_expert, n_active.reshape(1)


def _moe(x, h2p, idx, wts, gt, w_gate, b_gate, w_up, b_up, w_down, b_down, layer, n_tokens, rows_per_seq):
    r = x.shape[0]
    n_experts = w_gate.shape[1]
    n_rows_max = -(-(n_tokens * TOP_K + n_experts * (TM - 1)) // TM) * TM
    row_token, pos_flat, tile_expert, n_active = _dispatch_plan(idx, n_tokens, n_experts, n_rows_max)
    xs = _gather_rows(h2p, row_token, n_active, n_rows_max)
    hmid = _expert_up(xs, w_gate, w_up, b_gate, b_up, layer, tile_expert, n_active)
    y = _expert_down(hmid, w_down, b_down, layer, tile_expert, n_active)
    pos = jnp.zeros((r * TOP_K,), I32).at[:n_tokens * TOP_K].set(pos_flat)
    lane = jnp.arange(LANES)[None, :]
    wts = jnp.where((jnp.arange(r)[:, None] < n_tokens) & (lane < TOP_K), wts, 0.0)
    return _combine(y, pos, x, wts, gt, rows_per_seq)


def kernel(x_prompt, x_sample, c_prompt, c_sample, cache_k, cache_v, cache_logf, page_table, state_delta, state_delta_conv, state_sconv, w_ada, b_ada, g_mix, w_in, w_sconv, w_dconv, a_log, dt_bias, g_dnorm, g_qnorm, g_knorm, b_forget, w_oa, w_ob, w_oc, w_o, g_ffn, w_router, b_router, w_gate, b_gate, w_up, b_up, w_down, b_down):
    bp, seq, d = x_prompt.shape
    nb = x_sample.shape[0]
    assert nb == SUBLANES and x_sample.shape[1] == 1 and bp + nb <= 16
    n_layers = w_ada.shape[0]
    cw = w_sconv.shape[2]
    nh_d, dk, dv = state_delta.shape[2], state_delta.shape[3], state_delta.shape[4]
    nh_f, hd = cache_k.shape[3], cache_k.shape[4]
    page = cache_k.shape[2]
    assert dk == LANES and dv == LANES and hd == LANES and page == LANES
    assert 2 * nh_d + nh_f <= LANES and nh_f <= HEAD_ROWS
    rp = bp * seq
    r = rp + TM
    tiles_per_seq = seq // TM
    n_tokens = rp + nb
    qk_d, v_d = nh_d * dk, nh_d * dv
    f_d = nh_f * hd
    tq = 512 if seq % 512 == 0 else TM

    sizes = (cw, cw, cw, qk_d, qk_d, v_d, v_d, nh_d, nh_d, f_d, f_d, f_d, nh_f, d, d, d)
    off = np.concatenate([[0], np.cumsum(sizes)]).tolist()
    dz_col0 = off[6]

    x = jnp.concatenate([x_prompt.reshape(rp, d), x_sample.reshape(nb, d), jnp.zeros((TM - nb, d), F32)], axis=0)
    c_rows = jnp.concatenate([c_prompt, c_sample, jnp.zeros((16 - bp - nb, d), F32)], axis=0)
    mod_all = _ada(c_rows, w_ada, b_ada)
    logf_hm = jnp.transpose(cache_logf, (0, 1, 3, 2))
    k_hm = jnp.transpose(cache_k, (0, 1, 3, 2, 4))
    w_in_t = jnp.transpose(w_in, (2, 0, 1))
    v_hm = jnp.transpose(cache_v, (0, 1, 3, 2, 4))
    zero_state = jnp.zeros((bp, nh_d, dk, dv), F32)

    def with_sample(prompt_rows, sample_rows):
        pad = jnp.zeros((TM - nb, prompt_rows.shape[1]), prompt_rows.dtype)
        return jnp.concatenate([prompt_rows, sample_rows.astype(prompt_rows.dtype), pad], axis=0)

    def to_chunks(rows, n_rows=DELTA_CHUNK):
        return jnp.pad(rows[:, None, :], ((0, 0), (0, n_rows - 1), (0, 0))).reshape(nb * n_rows, -1)

    def head_rows(a):
        return jnp.pad(a, ((0, 0), (0, HEAD_ROWS - nh_f), (0, 0)))

    outs_p, outs_s = [], []
    for l in range(n_layers):
        mod = mod_all[l]
        modx = jnp.concatenate([jnp.broadcast_to(mod[:bp, None, :], (bp, SUBLANES, 6 * d)),
                                mod[None, bp:bp + nb, :]], axis=0)
        sh_m, sc_m, gt_m, sh_f, sc_f, gt_f = jnp.split(modx, 6, axis=-1)

        h = _norm(x, g_mix[l], sc_m, sh_m, tiles_per_seq)
        z_a = _in_proj(h, w_in_t, l, 0, off[7], False, F32)
        z_f = _in_proj(h, w_in_t, l, off[9], 3 * f_d, False, F32)
        gates = _in_proj(h, w_in_t, l, off[13], 3 * d, True, BF16)
        act = _in_small(h, w_in_t, l, off[7], off[12], a_log[l], dt_bias[l], b_forget[l])
        z_a_s = z_a[rp:rp + nb]
        act_s = act[rp:rp + nb]

        ya_p, sconv_p = _conv_prompt(z_a, w_sconv[l], 0, cw, bp, seq, True, BF16)
        ya_s, sconv_s = _conv_step(z_a_s, w_sconv[l], state_sconv[l], 0, cw, True, BF16)
        taps_a = w_sconv.shape[1]

        qkv_p, dconv_p = _conv_prompt(z_a, w_dconv[l], off[3], 2 * qk_d + v_d, bp, seq, False, F32)
        qkv_s, dconv_s = _conv_step(z_a_s, w_dconv[l], state_delta_conv[l], off[3], 2 * qk_d + v_d, False, F32)
        taps_b = w_dconv.shape[1]
        bb, cb = _delta_prep(act, nh_d)
        yb_p, delta_p = _delta(qkv_p, bb, cb, z_a, dz_col0, zero_state, g_dnorm[l], bp, seq)
        act_s_rows = to_chunks(jnp.where(jnp.arange(LANES)[None, :] < 2 * nh_d, act_s, 0.0))
        bb_s, cb_s = _delta_prep(act_s_rows, nh_d)
        yb_s_chunks, delta_s = _delta(to_chunks(qkv_s), bb_s, cb_s, to_chunks(z_a_s[:, dz_col0:dz_col0 + v_d]), 0,
                                      state_delta[l], g_dnorm[l], nb, DELTA_CHUNK)
        yb_s = yb_s_chunks[::DELTA_CHUNK]

        qn, kn, knb, fcum, k_hm_p, v_hm_p = _qkprep(z_f, act, g_qnorm[l], g_knorm[l], nh_f, bp, tiles_per_seq)
        yc_p = _flash(qn, knb, v_hm_p, fcum, nh_f, 2 * nh_d, bp, seq, tq)
        q_s = qn[rp:rp + nb].astype(F32).reshape(nb, nh_f, hd)
        k_s = kn[rp:rp + nb].reshape(nb, nh_f, hd)
        v_s = z_f[rp:rp + nb, 2 * f_d:].reshape(nb, nh_f, hd)
        logf_s = act_s[:, 2 * nh_d:2 * nh_d + nh_f]
        logf_s_b = jnp.broadcast_to(head_rows(logf_s[:, :, None]), (nb, HEAD_ROWS, page))
        yc_s = _decode_attention(head_rows(q_s), head_rows(k_s), head_rows(v_s), logf_s_b,
                                 logf_hm, k_hm, v_hm, l, page_table)[:, :nh_f].reshape(nb, f_d)

        ya = with_sample(ya_p, ya_s)
        yb = with_sample(yb_p, yb_s)
        yc = with_sample(yc_p, yc_s)
        merged = _merge(ya, yb, yc, w_oa, w_ob, w_oc, l, gates)
        x = _residual_mm(merged, w_o, l, x, gt_m, tiles_per_seq)

        h2p, idx, wts = _norm_router(x, g_ffn[l], sc_f, sh_f, w_router[l], b_router[l], tiles_per_seq)
        x = _moe(x, h2p, idx, wts, gt_f, w_gate, b_gate, w_up, b_up, w_down, b_down, l, n_tokens, seq)

        outs_p.append((jnp.transpose(k_hm_p, (0, 2, 1, 3)),
                       jnp.transpose(v_hm_p, (0, 2, 1, 3)),
                       act[:rp, 2 * nh_d:2 * nh_d + nh_f].reshape(bp, seq, nh_f),
                       delta_p,
                       dconv_p[:, SUBLANES - (taps_b - 1):],
                       sconv_p[:, SUBLANES - (taps_a - 1):]))
        outs_s.append((k_s.reshape(nb, 1, nh_f, hd), v_s.reshape(nb, 1, nh_f, hd), logf_s.reshape(nb, 1, nh_f),
                       delta_s, dconv_s, sconv_s))

    def stack(sts, i):
        return jnp.stack([st[i] for st in sts], axis=0)

    return (x[:rp].reshape(bp, seq, d), x[rp:rp + nb].reshape(nb, 1, d),
            stack(outs_p, 0), stack(outs_p, 1), stack(outs_p, 2), stack(outs_p, 3), stack(outs_p, 4), stack(outs_p, 5),
            stack(outs_s, 0), stack(outs_s, 1), stack(outs_s, 2), stack(outs_s, 3), stack(outs_s, 4), stack(outs_s, 5))
```
